```python
import jax, jax.numpy as jnp
from jax import lax
import numpy as np

D_MODEL = 1024
BATCH = 8
SEQ = 8192
DEPTH = 1

CHUNK = 64
EPS = 1e-6
ADA_SLOTS = 9
N_NORMS = 6
D_FF = 2816
FFN_RES_W = 0.5
CONV_WIDTH = D_MODEL
CONV_K = 3
HG_HEADS = 8
HG_DK = D_MODEL // HG_HEADS
HG_DV = D_MODEL // HG_HEADS
HG_WIDTH = HG_HEADS * HG_DK
N_BRANCH = 2
MIX_WIDTHS = (CONV_WIDTH, CONV_WIDTH, CONV_WIDTH,
              HG_WIDTH, HG_WIDTH, HG_WIDTH, HG_WIDTH,
              D_MODEL, D_MODEL)
MIX_IN = sum(MIX_WIDTHS)
MIX_SPLITS = tuple(int(v) for v in np.cumsum(MIX_WIDTHS)[:-1])

kernel_name = "hybrid_conv_hgrn2_macaron_block"


def _rmsnorm(x, g):
    xf = x.astype(jnp.float32)
    xf = xf * lax.rsqrt(jnp.mean(xf * xf, axis=-1, keepdims=True) + EPS)
    return (xf * g.astype(jnp.float32)).astype(x.dtype)


def _modulate(h, shift, scale):
    return h * (1.0 + scale[:, None, :]) + shift[:, None, :]


def _swiglu(h, w_in, w_out):
    a, b = jnp.split(h @ w_in, 2, axis=-1)
    return (jax.nn.silu(a) * b) @ w_out


def _short_conv(u, w, b):
    s = u.shape[1]
    up = jnp.pad(u, ((0, 0), (CONV_K - 1, 0), (0, 0)))
    y = b[None, None, :]
    for j in range(CONV_K):
        y = y + w[j][None, None, :] * up[:, j:j + s, :]
    return y


def _to_chunks(t):
    b, s, h, d = t.shape
    return t.reshape(b, s // CHUNK, CHUNK, h, d).transpose(1, 0, 3, 2, 4)


def _hgrn2_chunkwise(q, k, v, log_f):
    bsz, s, h, _ = q.shape
    qc, kc, vc = _to_chunks(q), _to_chunks(k), _to_chunks(v)
    ac = jnp.cumsum(_to_chunks(log_f), axis=3)
    causal = jnp.tril(jnp.ones((CHUNK, CHUNK), dtype=bool))[:, :, None]

    def step(state, inp):
        q_i, k_i, v_i, a_i = inp
        o_inter = jnp.einsum('bhtk,bhkv->bhtv', q_i * jnp.exp(a_i), state)
        diff = a_i[:, :, :, None, :] - a_i[:, :, None, :, :]
        decay = jnp.exp(jnp.where(causal, diff, -jnp.inf))
        scores = jnp.einsum('bhtk,bhtsk,bhsk->bhts', q_i, decay, k_i)
        o_intra = jnp.einsum('bhts,bhsv->bhtv', scores, v_i)
        a_last = a_i[:, :, -1:, :]
        k_dec = k_i * jnp.exp(a_last - a_i)
        state = (jnp.exp(a_last[:, :, 0, :])[..., None] * state
                 + jnp.einsum('bhsk,bhsv->bhkv', k_dec, v_i))
        return state, o_inter + o_intra

    s0 = jnp.zeros((bsz, h, HG_DK, HG_DV), jnp.float32)
    _, oc = lax.scan(step, s0, (qc, kc, vc, ac))
    return oc.transpose(1, 0, 3, 2, 4).reshape(bsz, s, h, HG_DV)


def _mixer(h, w_mix_in, conv_w, conv_b, w_conv_out, hg_norm_g, lb, w_hg_out, w_mix_out):
    bsz, s, _ = h.shape
    u = h @ w_mix_in
    cb, cc, cv, hq, hf, hi, hg, ga, gb = jnp.split(u, MIX_SPLITS, axis=-1)

    ya = (cb * _short_conv(cc * cv, conv_w, conv_b)) @ w_conv_out

    hf32 = hf.astype(jnp.float32)
    lb32 = lb.astype(jnp.float32)
    log_f = jnp.logaddexp(jnp.log(lb32), jnp.log1p(-lb32) + jax.nn.log_sigmoid(hf32))
    k = -jnp.expm1(log_f)
    q = jax.nn.silu(hq.astype(jnp.float32))
    shp = (bsz, s, HG_HEADS, HG_DK)
    o = _hgrn2_chunkwise(q.reshape(shp), k.reshape(shp),
                         hi.astype(jnp.float32).reshape(bsz, s, HG_HEADS, HG_DV),
                         log_f.reshape(shp))
    o = _rmsnorm(o, hg_norm_g.reshape(HG_HEADS, HG_DV)).reshape(bsz, s, HG_WIDTH)
    o = (o * jax.nn.silu(hg.astype(jnp.float32))).astype(h.dtype)
    yb = o @ w_hg_out

    m = jax.nn.sigmoid(ga) * ya + jax.nn.sigmoid(gb) * yb
    return m @ w_mix_out


def setup_inputs(seed: int = 0) -> dict:
    key = jax.random.key(seed)
    ks = jax.random.split(key, 17)
    L, D = DEPTH, D_MODEL

    def nrm(k, shape, scale):
        return jax.random.normal(k, shape, jnp.float32) * scale

    return {
        "x": nrm(ks[0], (BATCH, SEQ, D), 1.0),
        "c": nrm(ks[1], (BATCH, D), 1.0),
        "w_ada": nrm(ks[2], (L, D, ADA_SLOTS * D), D ** -0.5),
        "b_ada": nrm(ks[3], (L, ADA_SLOTS * D), 0.02),
        "norm_gains": 1.0 + nrm(ks[4], (L, N_NORMS, D), 0.05),
        "w_ffn1_in": nrm(ks[5], (L, D, 2 * D_FF), D ** -0.5),
        "w_ffn1_out": nrm(ks[6], (L, D_FF, D), D_FF ** -0.5),
        "w_mix_in": nrm(ks[7], (L, D, MIX_IN), D ** -0.5),
        "conv_w": nrm(ks[8], (L, CONV_K, CONV_WIDTH), CONV_K ** -0.5),
        "conv_b": nrm(ks[9], (L, CONV_WIDTH), 0.02),
        "w_conv_out": nrm(ks[10], (L, CONV_WIDTH, D), CONV_WIDTH ** -0.5),
        "hg_norm_g": 1.0 + nrm(ks[11], (L, HG_WIDTH), 0.05),
        "lb_logits": nrm(ks[12], (L + 1, HG_WIDTH), 1.0),
        "w_hg_out": nrm(ks[13], (L, HG_WIDTH, D), HG_WIDTH ** -0.5),
        "w_mix_out": nrm(ks[14], (L, D, D), D ** -0.5),
        "w_ffn2_in": nrm(ks[15], (L, D, 2 * D_FF), D ** -0.5),
        "w_ffn2_out": nrm(ks[16], (L, D_FF, D), D_FF ** -0.5),
    }


def reference(x, c, w_ada, b_ada, norm_gains, w_ffn1_in, w_ffn1_out, w_mix_in, conv_w,
              conv_b, w_conv_out, hg_norm_g, lb_logits, w_hg_out, w_mix_out,
              w_ffn2_in, w_ffn2_out):
    lb_all = jnp.cumsum(jax.nn.softmax(lb_logits.astype(jnp.float32), axis=0), axis=0)
    lb_all = lb_all.astype(x.dtype)
    c_act = jax.nn.silu(c)
    for l in range(DEPTH):
        ada = c_act @ w_ada[l] + b_ada[l]
        sh1, sc1, g1, sh2, sc2, g2, sh3, sc3, g3 = jnp.split(ada, ADA_SLOTS, axis=-1)
        ng = norm_gains[l]

        h = _modulate(_rmsnorm(x, ng[0]), sh1, sc1)
        y = _rmsnorm(_swiglu(h, w_ffn1_in[l], w_ffn1_out[l]), ng[1])
        x = x + FFN_RES_W * g1[:, None, :] * y

        h = _modulate(_rmsnorm(x, ng[2]), sh2, sc2)
        y = _mixer(h, w_mix_in[l], conv_w[l], conv_b[l], w_conv_out[l], hg_norm_g[l],
                   lb_all[l], w_hg_out[l], w_mix_out[l])
        y = _rmsnorm(y, ng[3])
        x = x + g2[:, None, :] * y

        h = _modulate(_rmsnorm(x, ng[4]), sh3, sc3)
        y = _rmsnorm(_swiglu(h, w_ffn2_in[l], w_ffn2_out[l]), ng[5])
        x = x + FFN_RES_W * g3[:, None, :] * y
    return x
```

```python
import functools

import jax
import jax.numpy as jnp
from jax import lax
from jax.experimental import pallas as pl
from jax.experimental.pallas import tpu as pltpu

EPS = 1e-6
FFN_RES_W = 0.5
N_HEADS = 8
ADA_SLOTS = 9
CONV_K = 3
N_MIX_GROUPS = 9

F32 = jnp.float32
BF16 = jnp.bfloat16

LANES = 128
SUBLANES = 8
MXU_TILE = 256

FFN_TM = 512
FFN_TF = 256
MIX_TM = 256
MIX_CHUNK = 64
VMEM_LIMIT = 56 * 1024 * 1024


def _sigmoid(x):
    return 1.0 / (1.0 + jnp.exp(-x))


def _rms_scale(x):
    return lax.rsqrt(jnp.mean(x * x, axis=-1, keepdims=True) + EPS)


def _resident(shape):
    nd = len(shape)
    return pl.BlockSpec(shape, lambda *_: (0,) * nd, pipeline_mode=pl.Buffered(1))


def _ada_kernel(c_ref, w_ref, b_ref, o_ref):
    c = c_ref[...]
    ca = (c * _sigmoid(c)).astype(BF16)
    o_ref[...] = jnp.dot(ca, w_ref[...].astype(BF16), preferred_element_type=F32) + b_ref[...]


def _ada(c, w, b):
    bsz, d = c.shape
    n = w.shape[1]
    tn = d
    return pl.pallas_call(
        _ada_kernel,
        grid=(n // tn,),
        in_specs=[
            pl.BlockSpec((bsz, d), lambda j: (0, 0)),
            pl.BlockSpec((d, tn), lambda j: (0, j)),
            pl.BlockSpec((1, tn), lambda j: (0, j)),
        ],
        out_specs=pl.BlockSpec((bsz, tn), lambda j: (0, j)),
        out_shape=jax.ShapeDtypeStruct((bsz, n), F32),
        compiler_params=pltpu.CompilerParams(dimension_semantics=("arbitrary",)),
        name="ada_proj",
    )(c, w, b.reshape(1, n))


def _ffn_kernel(x_ref, ada_ref, gpre_ref, gpost_ref, win_ref, wout_ref, o_ref, g_scr, *, slot, d_ff, tf):
    x = x_ref[0]
    shift = ada_ref[0, 3 * slot:3 * slot + 1, :]
    scale = ada_ref[0, 3 * slot + 1:3 * slot + 2, :]
    gate = ada_ref[0, 3 * slot + 2:3 * slot + 3, :]
    h = x * _rms_scale(x) * gpre_ref[...]
    h = h * (1.0 + scale) + shift
    hb = h.astype(BF16)
    for j in range(d_ff // tf):
        a = jnp.dot(hb, win_ref[:, j * tf:(j + 1) * tf], preferred_element_type=F32)
        b = jnp.dot(hb, win_ref[:, d_ff + j * tf:d_ff + (j + 1) * tf], preferred_element_type=F32)
        g_scr[:, j * tf:(j + 1) * tf] = (a * _sigmoid(a) * b).astype(BF16)
    y = jnp.dot(g_scr[...], wout_ref[...], preferred_element_type=F32)
    y = y * _rms_scale(y) * gpost_ref[...]
    o_ref[0] = x + (FFN_RES_W * gate) * y


def _ffn(x, ada, g_pre, g_post, w_in, w_out, slot):
    bsz, seq, d = x.shape
    d_ff = w_out.shape[0]
    tm = min(FFN_TM, seq)
    tf = FFN_TF
    assert seq % tm == 0 and d_ff % tf == 0
    kern = functools.partial(_ffn_kernel, slot=slot, d_ff=d_ff, tf=tf)
    return pl.pallas_call(
        kern,
        grid=(bsz, seq // tm),
        in_specs=[
            pl.BlockSpec((1, tm, d), lambda b, s: (b, s, 0)),
            pl.BlockSpec((1, ADA_SLOTS, d), lambda b, s: (b, 0, 0)),
            pl.BlockSpec((1, d), lambda b, s: (0, 0)),
            pl.BlockSpec((1, d), lambda b, s: (0, 0)),
            _resident((d, 2 * d_ff)),
            _resident((d_ff, d)),
        ],
        out_specs=pl.BlockSpec((1, tm, d), lambda b, s: (b, s, 0)),
        out_shape=jax.ShapeDtypeStruct(x.shape, F32),
        scratch_shapes=[pltpu.VMEM((tm, d_ff), BF16)],
        compiler_params=pltpu.CompilerParams(
            dimension_semantics=("parallel", "parallel"), vmem_limit_bytes=VMEM_LIMIT),
        name=f"ffn{slot}",
    )(x, ada, g_pre, g_post, w_in, w_out)


_NT = (((1,), (1,)), ((), ()))
_TN = (((0,), (0,)), ((), ()))


def _hgrn2_chunk(a8, q, k, v, st):
    c = a8.shape[0]
    nblk = c // SUBLANES
    blk = lambda t, j: t[j * SUBLANES:(j + 1) * SUBLANES]
    qs = [blk(q, j) for j in range(nblk)]
    ks = [blk(k, j) for j in range(nblk)]
    vs = [blk(v, j) for j in range(nblk)]
    acc = [blk(a8, j) for j in range(nblk)]
    tot = [t[SUBLANES - 1:SUBLANES] for t in acc]
    rowi = lax.broadcasted_iota(jnp.int32, (SUBLANES, 1), 0)

    o = []
    for j in range(nblk):
        oj = jnp.sum(qs[j] * ks[j], axis=-1, keepdims=True) * vs[j]
        for dlt in range(1, SUBLANES):
            w = jnp.exp(acc[j] - pltpu.roll(acc[j], dlt, 0))
            s = jnp.sum(qs[j] * w * pltpu.roll(ks[j], dlt, 0), axis=-1, keepdims=True)
            oj = oj + jnp.where(rowi >= dlt, s, 0.0) * pltpu.roll(vs[j], dlt, 0)
        o.append(oj)

    b = SUBLANES
    while b < c:
        nb = b // SUBLANES
        half = c // 2
        odd = [j for j in range(nblk) if (j // nb) % 2 == 1]
        even = [j for j in range(nblk) if (j // nb) % 2 == 0]
        qc = jnp.concatenate([qs[j] * jnp.exp(acc[j]) for j in odd], axis=0).astype(BF16)
        kc = jnp.concatenate([ks[j] * jnp.exp(tot[j // nb] - acc[j]) for j in even], axis=0).astype(BF16)
        vc = jnp.concatenate([vs[j] for j in even], axis=0).astype(BF16)
        sc = lax.dot_general(qc, kc, _NT, preferred_element_type=F32)
        ti = lax.broadcasted_iota(jnp.int32, (half, half), 0)
        si = lax.broadcasted_iota(jnp.int32, (half, half), 1)
        shift = b.bit_length() - 1
        same = lax.shift_right_logical(ti, shift) == lax.shift_right_logical(si, shift)
        sc = jnp.where(same, sc, 0.0).astype(BF16)
        contrib = jnp.dot(sc, vc, preferred_element_type=F32)
        for i, j in enumerate(odd):
            o[j] = o[j] + blk(contrib, i)
        acc = [acc[j] + tot[j // nb - 1] if (j // nb) % 2 == 1 else acc[j] for j in range(nblk)]
        tot = [tot[2 * m] + tot[2 * m + 1] for m in range(len(tot) // 2)]
        b *= 2

    a_c = jnp.concatenate(acc, axis=0)
    t_c = tot[0]
    q_state = (q * jnp.exp(a_c)).astype(BF16)
    k_state = (k * jnp.exp(t_c - a_c)).astype(BF16)
    o_inter = lax.dot_general(q_state, st.astype(BF16), _NT, preferred_element_type=F32)
    o_all = jnp.concatenate(o, axis=0) + o_inter
    st_new = st * jnp.exp(t_c) + lax.dot_general(v.astype(BF16), k_state, _TN, preferred_element_type=F32)
    return o_all, st_new
def _mixer_kernel(x_ref, ada_ref, gpre_ref, gpost_ref, wmix_ref, convw_ref, convb_ref, wconv_ref,
                  hgn_ref, lbl_ref, whg_ref, wout_ref, o_ref,
                  state_scr, halo_scr, q_scr, k_scr, v_scr, a_scr, og_scr, ga_scr, gb_scr, ya_scr, o_scr,
                  *, layer, chunk):
    tm, d = q_scr.shape
    dk = d // N_HEADS
    s_idx = pl.program_id(1)

    @pl.when(s_idx == 0)
    def _():
        state_scr[...] = jnp.zeros_like(state_scr)
        halo_scr[...] = jnp.zeros_like(halo_scr)

    x = x_ref[0]
    shift = ada_ref[0, 3:4, :]
    scale = ada_ref[0, 4:5, :]
    gate = ada_ref[0, 5:6, :]
    h = x * _rms_scale(x) * gpre_ref[...]
    h = h * (1.0 + scale) + shift
    hb = h.astype(BF16)

    def proj(g):
        return jnp.dot(hb, wmix_ref[:, g * d:(g + 1) * d], preferred_element_type=F32)

    row = lax.broadcasted_iota(jnp.int32, (tm, d), 0)

    p = proj(1) * proj(2)
    halo = halo_scr[...]
    p1 = jnp.where(row == 0, halo[SUBLANES - 1:SUBLANES, :], pltpu.roll(p, 1, 0))
    p2 = jnp.where(row == 0, halo[SUBLANES - 2:SUBLANES - 1, :],
                   jnp.where(row == 1, halo[SUBLANES - 1:SUBLANES, :], pltpu.roll(p, 2, 0)))
    halo_scr[...] = p[tm - SUBLANES:, :]
    conv = convb_ref[...] + convw_ref[0:1, :] * p2 + convw_ref[1:2, :] * p1 + convw_ref[2:3, :] * p
    ya_in = (proj(0) * conv).astype(BF16)
    ya_scr[...] = jnp.dot(ya_in, wconv_ref[...], preferred_element_type=F32)

    lbl = lbl_ref[...]
    lbe = jnp.exp(lbl - jnp.max(lbl, axis=0, keepdims=True))
    lbs = lbe / jnp.sum(lbe, axis=0, keepdims=True)
    lb = jnp.sum(lbs[:layer + 1], axis=0, keepdims=True)
    hq = proj(3)
    q_scr[...] = hq * _sigmoid(hq)
    sf = _sigmoid(proj(4))
    k_scr[...] = (1.0 - lb) * (1.0 - sf)
    a_scr[...] = jnp.log(lb + (1.0 - lb) * sf)
    v_scr[...] = proj(5)
    hg = proj(6)
    og_scr[...] = hg * _sigmoid(hg)
    ga_scr[...] = _sigmoid(proj(7))
    gb_scr[...] = _sigmoid(proj(8))

    c = a_scr[...]
    for sh in (1, 2, 4):
        c = c + jnp.where(row % SUBLANES >= sh, pltpu.roll(c, sh, 0), 0.0)
    a_scr[...] = c

    for ci in range(tm // chunk):
        rows = pl.ds(ci * chunk, chunk)
        for hd in range(N_HEADS):
            cols = pl.ds(hd * dk, dk)
            o, st = _hgrn2_chunk(a_scr[rows, cols], q_scr[rows, cols], k_scr[rows, cols],
                                 v_scr[rows, cols], state_scr[hd])
            o_scr[rows, cols] = o
            state_scr[hd] = st

    for hd in range(N_HEADS):
        cols = pl.ds(hd * dk, dk)
        o = o_scr[:, cols]
        o_scr[:, cols] = o * _rms_scale(o)
    ob = (o_scr[...] * hgn_ref[...] * og_scr[...]).astype(BF16)
    yb = jnp.dot(ob, whg_ref[...], preferred_element_type=F32)
    m = (ga_scr[...] * ya_scr[...] + gb_scr[...] * yb).astype(BF16)
    y = jnp.dot(m, wout_ref[...], preferred_element_type=F32)
    y = y * _rms_scale(y) * gpost_ref[...]
    o_ref[0] = x + gate * y


def _mixer(x, ada, g_pre, g_post, w_mix_in, conv_w, conv_b, w_conv_out, hg_norm_g, lb_logits,
           w_hg_out, w_mix_out, layer):
    bsz, seq, d = x.shape
    tm = min(MIX_TM, seq)
    chunk = min(MIX_CHUNK, tm)
    dk = d // N_HEADS
    assert seq % tm == 0 and tm % chunk == 0 and dk == LANES
    kern = functools.partial(_mixer_kernel, layer=layer, chunk=chunk)
    tile = lambda dt: pltpu.VMEM((tm, d), dt)
    return pl.pallas_call(
        kern,
        grid=(bsz, seq // tm),
        in_specs=[
            pl.BlockSpec((1, tm, d), lambda b, s: (b, s, 0)),
            pl.BlockSpec((1, ADA_SLOTS, d), lambda b, s: (b, 0, 0)),
            pl.BlockSpec((1, d), lambda b, s: (0, 0)),
            pl.BlockSpec((1, d), lambda b, s: (0, 0)),
            _resident((d, N_MIX_GROUPS * d)),
            pl.BlockSpec((CONV_K, d), lambda b, s: (0, 0)),
            pl.BlockSpec((1, d), lambda b, s: (0, 0)),
            _resident((d, d)),
            pl.BlockSpec((1, d), lambda b, s: (0, 0)),
            pl.BlockSpec(lb_logits.shape, lambda b, s: (0, 0)),
            _resident((d, d)),
            _resident((d, d)),
        ],
        out_specs=pl.BlockSpec((1, tm, d), lambda b, s: (b, s, 0)),
        out_shape=jax.ShapeDtypeStruct(x.shape, F32),
        scratch_shapes=[
            pltpu.VMEM((N_HEADS, dk, dk), F32),
            pltpu.VMEM((SUBLANES, d), F32),
            tile(F32), tile(F32), tile(F32), tile(F32),
            tile(F32), tile(F32), tile(F32), tile(F32),
            tile(F32),
        ],
        compiler_params=pltpu.CompilerParams(
            dimension_semantics=("arbitrary", "arbitrary"), vmem_limit_bytes=VMEM_LIMIT),
        name="mixer",
    )(x, ada, g_pre, g_post, w_mix_in, conv_w, conv_b, w_conv_out, hg_norm_g, lb_logits,
      w_hg_out, w_mix_out)


def kernel(x, c, w_ada, b_ada, norm_gains, w_ffn1_in, w_ffn1_out, w_mix_in, conv_w, conv_b,
           w_conv_out, hg_norm_g, lb_logits, w_hg_out, w_mix_out, w_ffn2_in, w_ffn2_out):
    depth = w_ada.shape[0]
    bsz, _, d = x.shape
    for l in range(depth):
        ada = _ada(c, w_ada[l], b_ada[l]).reshape(bsz, ADA_SLOTS, d)
        ng = norm_gains[l]
        gain = lambda i: ng[i:i + 1]
        x = _ffn(x, ada, gain(0), gain(1), w_ffn1_in[l].astype(BF16), w_ffn1_out[l].astype(BF16), 0)
        x = _mixer(x, ada, gain(2), gain(3), w_mix_in[l].astype(BF16), conv_w[l], conv_b[l:l + 1],
                   w_conv_out[l].astype(BF16), hg_norm_g[l:l + 1], lb_logits,
                   w_hg_out[l].astype(BF16), w_mix_out[l].astype(BF16), l)
        x = _ffn(x, ada, gain(4), gain(5), w_ffn2_in[l].astype(BF16), w_ffn2_out[l].astype(BF16), 2)
    return x
```

```python
import functools
import math

import jax
import jax.numpy as jnp
from jax import lax
from jax.experimental import pallas as pl
from jax.experimental.pallas import tpu as pltpu

EPS = 1e-6
FFN_RES_W = 0.5
N_HEADS = 8
ADA_SLOTS = 9
CONV_K = 3
N_MIX_GROUPS = 9
NEG_LOG2E = -math.log2(math.e)

F32 = jnp.float32
BF16 = jnp.bfloat16

LANES = 128
SUBLANES = 8
MXU_TILE = 256

FFN_TM = 512
FFN_TF = 256
MIX_TM = 256
MIX_CHUNK = 64
VMEM_LIMIT = 56 * 1024 * 1024


def _sigmoid(x):
    return 1.0 / (1.0 + jnp.exp2(x * NEG_LOG2E))


def _rms_scale(x):
    return lax.rsqrt(jnp.mean(x * x, axis=-1, keepdims=True) + EPS)


def _resident(shape):
    nd = len(shape)
    return pl.BlockSpec(shape, lambda *_: (0,) * nd, pipeline_mode=pl.Buffered(1))


def _weight(w):
    return jnp.pad(w.astype(BF16), ((0, 0), (0, LANES)))


def _ada_kernel(c_ref, w_ref, b_ref, o_ref):
    c = c_ref[...]
    ca = (c * _sigmoid(c)).astype(BF16)
    o_ref[...] = jnp.dot(ca, w_ref[...].astype(BF16), preferred_element_type=F32) + b_ref[...]


def _ada(c, w, b):
    bsz, d = c.shape
    n = w.shape[1]
    tn = d
    return pl.pallas_call(
        _ada_kernel,
        grid=(n // tn,),
        in_specs=[
            pl.BlockSpec((bsz, d), lambda j: (0, 0)),
            pl.BlockSpec((d, tn), lambda j: (0, j)),
            pl.BlockSpec((1, tn), lambda j: (0, j)),
        ],
        out_specs=pl.BlockSpec((bsz, tn), lambda j: (0, j)),
        out_shape=jax.ShapeDtypeStruct((bsz, n), F32),
        compiler_params=pltpu.CompilerParams(dimension_semantics=("arbitrary",)),
        name="ada_proj",
    )(c, w, b.reshape(1, n))


def _ffn_kernel(x_ref, ada_ref, gpre_ref, gpost_ref, win_ref, wout_ref, o_ref, g_scr, *, slot, d_ff, tf):
    x = x_ref[0]
    d = x.shape[-1]
    shift = ada_ref[0, 3 * slot:3 * slot + 1, :]
    scale = ada_ref[0, 3 * slot + 1:3 * slot + 2, :]
    gate = ada_ref[0, 3 * slot + 2:3 * slot + 3, :]
    h = x * _rms_scale(x) * gpre_ref[...]
    h = h * (1.0 + scale) + shift
    hb = h.astype(BF16)
    for j in range(d_ff // tf):
        a = jnp.dot(hb, win_ref[:, j * tf:(j + 1) * tf], preferred_element_type=F32)
        b = jnp.dot(hb, win_ref[:, d_ff + j * tf:d_ff + (j + 1) * tf], preferred_element_type=F32)
        g_scr[:, j * tf:(j + 1) * tf] = (a * _sigmoid(a) * b).astype(BF16)
    y = jnp.dot(g_scr[...], wout_ref[:, :d], preferred_element_type=F32)
    y = y * _rms_scale(y) * gpost_ref[...]
    o_ref[0] = x + (FFN_RES_W * gate) * y


def _ffn(x, ada, g_pre, g_post, w_in, w_out, slot):
    bsz, seq, d = x.shape
    d_ff = w_out.shape[0]
    tm = min(FFN_TM, seq)
    tf = FFN_TF
    assert seq % tm == 0 and d_ff % tf == 0
    kern = functools.partial(_ffn_kernel, slot=slot, d_ff=d_ff, tf=tf)
    return pl.pallas_call(
        kern,
        grid=(bsz, seq // tm),
        in_specs=[
            pl.BlockSpec((1, tm, d), lambda b, s: (b, s, 0)),
            pl.BlockSpec((1, ADA_SLOTS, d), lambda b, s: (b, 0, 0)),
            pl.BlockSpec((1, d), lambda b, s: (0, 0)),
            pl.BlockSpec((1, d), lambda b, s: (0, 0)),
            _resident(w_in.shape),
            _resident(w_out.shape),
        ],
        out_specs=pl.BlockSpec((1, tm, d), lambda b, s: (b, s, 0)),
        out_shape=jax.ShapeDtypeStruct(x.shape, F32),
        scratch_shapes=[pltpu.VMEM((tm, d_ff), BF16)],
        compiler_params=pltpu.CompilerParams(
            dimension_semantics=("parallel", "parallel"), vmem_limit_bytes=VMEM_LIMIT),
        name=f"ffn{slot}",
    )(x, ada, g_pre, g_post, w_in, w_out)


_NT = (((1,), (1,)), ((), ()))
_TN = (((0,), (0,)), ((), ()))
PITCH = 2 * SUBLANES


def _block_rows(r0, c):
    return [2 * (r0 + SUBLANES * j) + SUBLANES for j in range(c // SUBLANES)]


def _level_split(nblk, nb):
    odd = [j for j in range(nblk) if (j // nb) % 2 == 1]
    even = [j for j in range(nblk) if (j // nb) % 2 == 0]
    return odd, even


def _hgrn2_scores(a_ref, q_ref, k_ref, v_ref, r0, hd, c):
    nblk = c // SUBLANES
    dk = q_ref.shape[-1] // N_HEADS
    base = _block_rows(r0, c)
    qs = [q_ref[pl.ds(r0 + SUBLANES * j, SUBLANES), pl.ds(hd * dk, dk)] for j in range(nblk)]
    ks = [k_ref[hd, pl.ds(base[j], SUBLANES), :] for j in range(nblk)]
    vs = [v_ref[hd, pl.ds(base[j], SUBLANES), :] for j in range(nblk)]
    acc = [a_ref[hd, pl.ds(base[j], SUBLANES), :] for j in range(nblk)]
    tot = [t[SUBLANES - 1:SUBLANES] for t in acc]
    half = c // 2
    ti = lax.broadcasted_iota(jnp.int32, (half, half), 0)
    si = lax.broadcasted_iota(jnp.int32, (half, half), 1)
    scores = []
    b = SUBLANES
    while b < c:
        nb = b // SUBLANES
        odd, even = _level_split(nblk, nb)
        qc = jnp.concatenate([qs[j] * jnp.exp2(acc[j]) for j in odd], axis=0).astype(BF16)
        kc = jnp.concatenate([ks[j] * jnp.exp2(tot[j // nb] - acc[j]) for j in even], axis=0).astype(BF16)
        sc = lax.dot_general(qc, kc, _NT, preferred_element_type=F32)
        shift = b.bit_length() - 1
        same = lax.shift_right_logical(ti, shift) == lax.shift_right_logical(si, shift)
        scores.append(jnp.where(same, sc, 0.0).astype(BF16))
        acc = [acc[j] + tot[j // nb - 1] if (j // nb) % 2 == 1 else acc[j] for j in range(nblk)]
        tot = [tot[2 * m] + tot[2 * m + 1] for m in range(len(tot) // 2)]
        b *= 2
    a_c = jnp.concatenate(acc, axis=0)
    t_c = tot[0]
    q_state = (jnp.concatenate(qs, axis=0) * jnp.exp2(a_c)).astype(BF16)
    k_state = (jnp.concatenate(ks, axis=0) * jnp.exp2(t_c - a_c)).astype(BF16)
    v_all = jnp.concatenate(vs, axis=0).astype(BF16)
    d_state = lax.dot_general(v_all, k_state, _TN, preferred_element_type=F32)
    return scores, q_state, d_state, jnp.exp2(t_c)


def _hgrn2_leaf(a_ref, q_ref, k_ref, v_ref, r0, hd, c):
    dk = q_ref.shape[-1] // N_HEADS
    o = []
    for j, row in enumerate(_block_rows(r0, c)):
        q = q_ref[pl.ds(r0 + SUBLANES * j, SUBLANES), pl.ds(hd * dk, dk)]
        data = pl.ds(row, SUBLANES)
        acc = a_ref[hd, data, :]
        oj = jnp.sum(q * k_ref[hd, data, :], axis=-1, keepdims=True) * v_ref[hd, data, :]
        for dlt in range(1, SUBLANES):
            sh = pl.ds(row - dlt, SUBLANES)
            w = jnp.exp2(acc - a_ref[hd, sh, :])
            s = jnp.sum(q * w * k_ref[hd, sh, :], axis=-1, keepdims=True)
            oj = oj + s * v_ref[hd, sh, :]
        o.append(oj)
    return jnp.concatenate(o, axis=0)


def _hgrn2_output(scores, q_state, st_bf16, v_ref, r0, hd, c):
    nblk = c // SUBLANES
    base = _block_rows(r0, c)
    vs = [v_ref[hd, pl.ds(base[j], SUBLANES), :] for j in range(nblk)]
    o_inter = lax.dot_general(q_state, st_bf16, _NT, preferred_element_type=F32)
    o = [o_inter[j * SUBLANES:(j + 1) * SUBLANES] for j in range(nblk)]
    b = SUBLANES
    for sc in scores:
        odd, even = _level_split(nblk, b // SUBLANES)
        vc = jnp.concatenate([vs[j] for j in even], axis=0).astype(BF16)
        contrib = jnp.dot(sc, vc, preferred_element_type=F32)
        for i, j in enumerate(odd):
            o[j] = o[j] + contrib[i * SUBLANES:(i + 1) * SUBLANES]
        b *= 2
    return jnp.concatenate(o, axis=0)


def _mixer_kernel(x_ref, ada_ref, gpre_ref, gpost_ref, wmix_ref, convw_ref, convb_ref, wconv_ref,
                  hgn_ref, lbl_ref, whg_ref, wout_ref, o_ref,
                  state_scr, halo_scr, q_scr, k_scr, v_scr, a_scr, og_scr, ga_scr, gb_scr, ya_scr, o_scr,
                  *, layer, chunk):
    tm, d = q_scr.shape
    dk = d // N_HEADS
    nblk = tm // SUBLANES
    items = [(ci, hd) for ci in range(tm // chunk) for hd in range(N_HEADS)]
    s_idx = pl.program_id(1)

    @pl.when(s_idx == 0)
    def _():
        state_scr[...] = jnp.zeros_like(state_scr)
        halo_scr[...] = jnp.zeros_like(halo_scr)
        zeros = jnp.zeros((N_HEADS, SUBLANES, dk), F32)
        for j in range(nblk):
            pad = pl.ds(PITCH * j, SUBLANES)
            a_scr[:, pad, :] = zeros
            k_scr[:, pad, :] = zeros
            v_scr[:, pad, :] = zeros

    x = x_ref[0]
    shift = ada_ref[0, 3:4, :]
    scale = ada_ref[0, 4:5, :]
    gate = ada_ref[0, 5:6, :]
    h = x * _rms_scale(x) * gpre_ref[...]
    h = h * (1.0 + scale) + shift
    hb = h.astype(BF16)

    def proj(g):
        return jnp.dot(hb, wmix_ref[:, g * d:(g + 1) * d], preferred_element_type=F32)

    def store_blocks(ref, val):
        for j in range(nblk):
            for hd in range(N_HEADS):
                ref[hd, pl.ds(PITCH * j + SUBLANES, SUBLANES), :] = (
                    val[j * SUBLANES:(j + 1) * SUBLANES, hd * dk:(hd + 1) * dk])

    row = lax.broadcasted_iota(jnp.int32, (tm, 1), 0)

    lbl = lbl_ref[...]
    lbe = jnp.exp(lbl - jnp.max(lbl, axis=0, keepdims=True))
    lbs = lbe / jnp.sum(lbe, axis=0, keepdims=True)
    lb = jnp.sum(lbs[:layer + 1], axis=0, keepdims=True)
    hq = proj(3)
    q_scr[...] = hq * _sigmoid(hq)
    sf = _sigmoid(proj(4))
    store_blocks(k_scr, (1.0 - lb) * (1.0 - sf))
    store_blocks(a_scr, jnp.log2(lb + (1.0 - lb) * sf))
    for sh in (1, 2, 4):
        for j in range(nblk):
            data = pl.ds(PITCH * j + SUBLANES, SUBLANES)
            a_scr[:, data, :] = a_scr[:, data, :] + a_scr[:, pl.ds(PITCH * j + SUBLANES - sh, SUBLANES), :]
    store_blocks(v_scr, proj(5))

    stage1 = {it: _hgrn2_scores(a_scr, q_scr, k_scr, v_scr, it[0] * chunk, it[1], chunk) for it in items}

    p = proj(1) * proj(2)
    halo = halo_scr[...]
    p1 = jnp.where(row == 0, halo[SUBLANES - 1:SUBLANES, :], pltpu.roll(p, 1, 0))
    p2 = jnp.where(row == 0, halo[SUBLANES - 2:SUBLANES - 1, :],
                   jnp.where(row == 1, halo[SUBLANES - 1:SUBLANES, :], pltpu.roll(p, 2, 0)))
    halo_scr[...] = p[tm - SUBLANES:, :]
    conv = convb_ref[...] + convw_ref[0:1, :] * p2 + convw_ref[1:2, :] * p1 + convw_ref[2:3, :] * p
    ya_in = (proj(0) * conv).astype(BF16)
    ya_scr[...] = jnp.dot(ya_in, wconv_ref[:, :d], preferred_element_type=F32)

    for ci, hd in items:
        o_scr[pl.ds(ci * chunk, chunk), pl.ds(hd * dk, dk)] = _hgrn2_leaf(
            a_scr, q_scr, k_scr, v_scr, ci * chunk, hd, chunk)
    chunk_state = {}
    for hd in range(N_HEADS):
        st = state_scr[hd]
        for ci in range(tm // chunk):
            chunk_state[ci, hd] = st.astype(BF16)
            _, _, d_state, decay = stage1[ci, hd]
            st = st * decay + d_state
        state_scr[hd] = st

    hg = proj(6)
    og_scr[...] = hg * _sigmoid(hg)

    for ci, hd in items:
        scores, q_state, _, _ = stage1[ci, hd]
        rows, cols = pl.ds(ci * chunk, chunk), pl.ds(hd * dk, dk)
        o_scr[rows, cols] = o_scr[rows, cols] + _hgrn2_output(
            scores, q_state, chunk_state[ci, hd], v_scr, ci * chunk, hd, chunk)

    ga_scr[...] = _sigmoid(proj(7))
    gb_scr[...] = _sigmoid(proj(8))

    for hd in range(N_HEADS):
        cols = pl.ds(hd * dk, dk)
        o = o_scr[:, cols]
        o_scr[:, cols] = o * _rms_scale(o)
    ob = (o_scr[...] * hgn_ref[...] * og_scr[...]).astype(BF16)
    yb = jnp.dot(ob, whg_ref[:, :d], preferred_element_type=F32)
    m = (ga_scr[...] * ya_scr[...] + gb_scr[...] * yb).astype(BF16)
    y = jnp.dot(m, wout_ref[:, :d], preferred_element_type=F32)
    y = y * _rms_scale(y) * gpost_ref[...]
    o_ref[0] = x + gate * y


def _mixer(x, ada, g_pre, g_post, w_mix_in, conv_w, conv_b, w_conv_out, hg_norm_g, lb_logits,
           w_hg_out, w_mix_out, layer):
    bsz, seq, d = x.shape
    tm = min(MIX_TM, seq)
    chunk = min(MIX_CHUNK, tm)
    dk = d // N_HEADS
    assert seq % tm == 0 and tm % chunk == 0 and dk == LANES
    kern = functools.partial(_mixer_kernel, layer=layer, chunk=chunk)
    tile = pltpu.VMEM((tm, d), F32)
    padded = pltpu.VMEM((N_HEADS, 2 * tm, dk), F32)
    return pl.pallas_call(
        kern,
        grid=(bsz, seq // tm),
        in_specs=[
            pl.BlockSpec((1, tm, d), lambda b, s: (b, s, 0)),
            pl.BlockSpec((1, ADA_SLOTS, d), lambda b, s: (b, 0, 0)),
            pl.BlockSpec((1, d), lambda b, s: (0, 0)),
            pl.BlockSpec((1, d), lambda b, s: (0, 0)),
            _resident(w_mix_in.shape),
            pl.BlockSpec((CONV_K, d), lambda b, s: (0, 0)),
            pl.BlockSpec((1, d), lambda b, s: (0, 0)),
            _resident(w_conv_out.shape),
            pl.BlockSpec((1, d), lambda b, s: (0, 0)),
            pl.BlockSpec(lb_logits.shape, lambda b, s: (0, 0)),
            _resident(w_hg_out.shape),
            _resident(w_mix_out.shape),
        ],
        out_specs=pl.BlockSpec((1, tm, d), lambda b, s: (b, s, 0)),
        out_shape=jax.ShapeDtypeStruct(x.shape, F32),
        scratch_shapes=[
            pltpu.VMEM((N_HEADS, dk, dk), F32),
            pltpu.VMEM((SUBLANES, d), F32),
            tile, padded, padded, padded,
            tile, tile, tile, tile,
            tile,
        ],
        compiler_params=pltpu.CompilerParams(
            dimension_semantics=("arbitrary", "arbitrary"), vmem_limit_bytes=VMEM_LIMIT),
        name="mixer",
    )(x, ada, g_pre, g_post, w_mix_in, conv_w, conv_b, w_conv_out, hg_norm_g, lb_logits,
      w_hg_out, w_mix_out)


def kernel(x, c, w_ada, b_ada, norm_gains, w_ffn1_in, w_ffn1_out, w_mix_in, conv_w, conv_b,
           w_conv_out, hg_norm_g, lb_logits, w_hg_out, w_mix_out, w_ffn2_in, w_ffn2_out):
    depth = w_ada.shape[0]
    bsz, _, d = x.shape
    for l in range(depth):
        ada = _ada(c, w_ada[l], b_ada[l]).reshape(bsz, ADA_SLOTS, d)
        ng = norm_gains[l]
        gain = lambda i: ng[i:i + 1]
        x = _ffn(x, ada, gain(0), gain(1), w_ffn1_in[l].astype(BF16), _weight(w_ffn1_out[l]), 0)
        x = _mixer(x, ada, gain(2), gain(3), _weight(w_mix_in[l]), conv_w[l], conv_b[l:l + 1],
                   _weight(w_conv_out[l]), hg_norm_g[l:l + 1], lb_logits,
                   _weight(w_hg_out[l]), _weight(w_mix_out[l]), l)
        x = _ffn(x, ada, gain(4), gain(5), w_ffn2_in[l].astype(BF16), _weight(w_ffn2_out[l]), 2)
    return x
```

```python
import functools
import math

import jax
import jax.numpy as jnp
from jax import lax
from jax.experimental import pallas as pl
from jax.experimental.pallas import tpu as pltpu

EPS = 1e-6
FFN_RES_W = 0.5
N_HEADS = 8
ADA_SLOTS = 9
CONV_K = 3
N_MIX_GROUPS = 9
NEG_LOG2E = -math.log2(math.e)

F32 = jnp.float32
BF16 = jnp.bfloat16

LANES = 128
SUBLANES = 8
MXU_TILE = 256

FFN_TM = 512
FFN_TF = 256
MIX_TM = 256
MIX_CHUNK = 128
VMEM_LIMIT = 56 * 1024 * 1024


def _sigmoid(x):
    return 1.0 / (1.0 + jnp.exp2(x * NEG_LOG2E))


def _rms_scale(x):
    return lax.rsqrt(jnp.mean(x * x, axis=-1, keepdims=True) + EPS)


def _resident(shape):
    nd = len(shape)
    return pl.BlockSpec(shape, lambda *_: (0,) * nd, pipeline_mode=pl.Buffered(1))


def _weight(w):
    return jnp.pad(w.astype(BF16), ((0, 0), (0, LANES)))


def _ada_kernel(c_ref, w_ref, b_ref, o_ref):
    c = c_ref[...]
    ca = (c * _sigmoid(c)).astype(BF16)
    o_ref[...] = jnp.dot(ca, w_ref[...].astype(BF16), preferred_element_type=F32) + b_ref[...]


def _ada(c, w, b):
    bsz, d = c.shape
    n = w.shape[1]
    tn = d
    return pl.pallas_call(
        _ada_kernel,
        grid=(n // tn,),
        in_specs=[
            pl.BlockSpec((bsz, d), lambda j: (0, 0)),
            pl.BlockSpec((d, tn), lambda j: (0, j)),
            pl.BlockSpec((1, tn), lambda j: (0, j)),
        ],
        out_specs=pl.BlockSpec((bsz, tn), lambda j: (0, j)),
        out_shape=jax.ShapeDtypeStruct((bsz, n), F32),
        compiler_params=pltpu.CompilerParams(dimension_semantics=("arbitrary",)),
        name="ada_proj",
    )(c, w, b.reshape(1, n))


def _ffn_kernel(x_ref, ada_ref, gpre_ref, gpost_ref, win_ref, wout_ref, o_ref, g_scr, y_scr,
                *, slot, d_ff, tf):
    x = x_ref[0]
    tm, d = x.shape
    shift = ada_ref[0, 3 * slot:3 * slot + 1, :]
    scale = ada_ref[0, 3 * slot + 1:3 * slot + 2, :]
    gate = ada_ref[0, 3 * slot + 2:3 * slot + 3, :]
    h = x * _rms_scale(x) * gpre_ref[...]
    h = h * (1.0 + scale) + shift
    hb = h.astype(BF16)
    for j in range(d_ff // tf):
        a = jnp.dot(hb, win_ref[:, j * tf:(j + 1) * tf], preferred_element_type=F32)
        b = jnp.dot(hb, win_ref[:, d_ff + j * tf:d_ff + (j + 1) * tf], preferred_element_type=F32)
        g_scr[:, j * tf:(j + 1) * tf] = (a * _sigmoid(a) * b).astype(BF16)
    ssq = jnp.zeros((tm, 1), F32)
    for j in range(0, d, MXU_TILE):
        y = jnp.dot(g_scr[...], wout_ref[:, j:j + MXU_TILE], preferred_element_type=F32)
        y_scr[:, j:j + MXU_TILE] = y
        ssq = ssq + jnp.sum(y * y, axis=-1, keepdims=True)
    o_ref[0] = x + (FFN_RES_W * gate * gpost_ref[...]) * (y_scr[...] * lax.rsqrt(ssq * (1.0 / d) + EPS))


def _ffn(x, ada, g_pre, g_post, w_in, w_out, slot):
    bsz, seq, d = x.shape
    d_ff = w_out.shape[0]
    tm = min(FFN_TM, seq)
    tf = FFN_TF
    assert seq % tm == 0 and d_ff % tf == 0 and d % MXU_TILE == 0
    kern = functools.partial(_ffn_kernel, slot=slot, d_ff=d_ff, tf=tf)
    return pl.pallas_call(
        kern,
        grid=(bsz, seq // tm),
        in_specs=[
            pl.BlockSpec((1, tm, d), lambda b, s: (b, s, 0)),
            pl.BlockSpec((1, ADA_SLOTS, d), lambda b, s: (b, 0, 0)),
            pl.BlockSpec((1, d), lambda b, s: (0, 0)),
            pl.BlockSpec((1, d), lambda b, s: (0, 0)),
            _resident(w_in.shape),
            _resident(w_out.shape),
        ],
        out_specs=pl.BlockSpec((1, tm, d), lambda b, s: (b, s, 0)),
        out_shape=jax.ShapeDtypeStruct(x.shape, F32),
        scratch_shapes=[pltpu.VMEM((tm, d_ff), BF16), pltpu.VMEM((tm, d), F32)],
        compiler_params=pltpu.CompilerParams(
            dimension_semantics=("parallel", "parallel"), vmem_limit_bytes=VMEM_LIMIT),
        name=f"ffn{slot}",
    )(x, ada, g_pre, g_post, w_in, w_out)


_NT = (((1,), (1,)), ((), ()))
_TN = (((0,), (0,)), ((), ()))
PITCH = 2 * SUBLANES


def _block_rows(r0, c):
    return [2 * (r0 + SUBLANES * j) + SUBLANES for j in range(c // SUBLANES)]


def _level_split(nblk, nb):
    odd = [j for j in range(nblk) if (j // nb) % 2 == 1]
    even = [j for j in range(nblk) if (j // nb) % 2 == 0]
    return odd, even


def _hgrn2_scores(a_ref, q_ref, k_ref, v_ref, r0, hd, c):
    nblk = c // SUBLANES
    dk = q_ref.shape[-1] // N_HEADS
    base = _block_rows(r0, c)
    qs = [q_ref[pl.ds(r0 + SUBLANES * j, SUBLANES), pl.ds(hd * dk, dk)] for j in range(nblk)]
    ks = [k_ref[hd, pl.ds(base[j], SUBLANES), :] for j in range(nblk)]
    vs = [v_ref[hd, pl.ds(base[j], SUBLANES), :] for j in range(nblk)]
    acc = [a_ref[hd, pl.ds(base[j], SUBLANES), :] for j in range(nblk)]
    tot = [t[SUBLANES - 1:SUBLANES] for t in acc]
    half = c // 2
    ti = lax.broadcasted_iota(jnp.int32, (half, half), 0)
    si = lax.broadcasted_iota(jnp.int32, (half, half), 1)
    scores = []
    b = SUBLANES
    while b < c:
        nb = b // SUBLANES
        odd, even = _level_split(nblk, nb)
        qc = jnp.concatenate([qs[j] * jnp.exp2(acc[j]) for j in odd], axis=0).astype(BF16)
        kc = jnp.concatenate([ks[j] * jnp.exp2(tot[j // nb] - acc[j]) for j in even], axis=0).astype(BF16)
        sc = lax.dot_general(qc, kc, _NT, preferred_element_type=F32)
        shift = b.bit_length() - 1
        same = lax.shift_right_logical(ti, shift) == lax.shift_right_logical(si, shift)
        scores.append(jnp.where(same, sc, 0.0).astype(BF16))
        acc = [acc[j] + tot[j // nb - 1] if (j // nb) % 2 == 1 else acc[j] for j in range(nblk)]
        tot = [tot[2 * m] + tot[2 * m + 1] for m in range(len(tot) // 2)]
        b *= 2
    a_c = jnp.concatenate(acc, axis=0)
    t_c = tot[0]
    q_state = (jnp.concatenate(qs, axis=0) * jnp.exp2(a_c)).astype(BF16)
    k_state = (jnp.concatenate(ks, axis=0) * jnp.exp2(t_c - a_c)).astype(BF16)
    v_all = jnp.concatenate(vs, axis=0).astype(BF16)
    d_state = lax.dot_general(v_all, k_state, _TN, preferred_element_type=F32)
    return scores, q_state, d_state, jnp.exp2(t_c)


def _hgrn2_leaf(a_ref, q_ref, k_ref, v_ref, r0, hd, c):
    dk = q_ref.shape[-1] // N_HEADS
    o = []
    for j, row in enumerate(_block_rows(r0, c)):
        q = q_ref[pl.ds(r0 + SUBLANES * j, SUBLANES), pl.ds(hd * dk, dk)]
        data = pl.ds(row, SUBLANES)
        acc = a_ref[hd, data, :]
        oj = jnp.sum(q * k_ref[hd, data, :], axis=-1, keepdims=True) * v_ref[hd, data, :]
        for dlt in range(1, SUBLANES):
            sh = pl.ds(row - dlt, SUBLANES)
            w = jnp.exp2(acc - a_ref[hd, sh, :])
            s = jnp.sum(q * w * k_ref[hd, sh, :], axis=-1, keepdims=True)
            oj = oj + s * v_ref[hd, sh, :]
        o.append(oj)
    return jnp.concatenate(o, axis=0)


def _hgrn2_output(scores, q_state, st_bf16, v_ref, r0, hd, c):
    nblk = c // SUBLANES
    base = _block_rows(r0, c)
    vs = [v_ref[hd, pl.ds(base[j], SUBLANES), :] for j in range(nblk)]
    o_inter = lax.dot_general(q_state, st_bf16, _NT, preferred_element_type=F32)
    o = [o_inter[j * SUBLANES:(j + 1) * SUBLANES] for j in range(nblk)]
    b = SUBLANES
    for sc in scores:
        odd, even = _level_split(nblk, b // SUBLANES)
        vc = jnp.concatenate([vs[j] for j in even], axis=0).astype(BF16)
        contrib = jnp.dot(sc, vc, preferred_element_type=F32)
        for i, j in enumerate(odd):
            o[j] = o[j] + contrib[i * SUBLANES:(i + 1) * SUBLANES]
        b *= 2
    return jnp.concatenate(o, axis=0)


def _mixer_kernel(x_ref, ada_ref, gpre_ref, gpost_ref, wmix_ref, convw_ref, convb_ref, wconv_ref,
                  hgn_ref, lbl_ref, whg_ref, wout_ref, o_ref,
                  state_scr, q_scr, k_scr, v_scr, a_scr, p_scr, og_scr, ga_scr, gb_scr, ya_scr, o_scr,
                  y_scr, hb_scr, yain_scr, ob_scr, m_scr, *, layer, chunk):
    tm, d = q_scr.shape
    dk = d // N_HEADS
    nblk = tm // SUBLANES
    n_chunks = tm // chunk
    items = [(ci, hd) for ci in range(n_chunks) for hd in range(N_HEADS)]
    col_tiles = [pl.ds(j, MXU_TILE) for j in range(0, d, MXU_TILE)]
    heads_per_tile = MXU_TILE // dk
    s_idx = pl.program_id(1)

    @pl.when(s_idx == 0)
    def _():
        state_scr[...] = jnp.zeros_like(state_scr)
        zeros = jnp.zeros((N_HEADS, SUBLANES, dk), F32)
        p_scr[:, pl.ds(0, SUBLANES), :] = zeros
        for j in range(nblk):
            pad = pl.ds(PITCH * j, SUBLANES)
            a_scr[:, pad, :] = zeros
            k_scr[:, pad, :] = zeros
            v_scr[:, pad, :] = zeros

    x = x_ref[0]
    shift = ada_ref[0, 3:4, :]
    scale = ada_ref[0, 4:5, :]
    gate = ada_ref[0, 5:6, :]
    h = x * _rms_scale(x) * gpre_ref[...]
    hb_scr[...] = (h * (1.0 + scale) + shift).astype(BF16)

    def proj(g, t):
        cols = pl.ds(g * d + t * MXU_TILE, MXU_TILE)
        return jnp.dot(hb_scr[...], wmix_ref[:, cols], preferred_element_type=F32)

    def store_blocks(ref, t, val):
        for i in range(heads_per_tile):
            hd = t * heads_per_tile + i
            for j in range(nblk):
                ref[hd, pl.ds(PITCH * j + SUBLANES, SUBLANES), :] = (
                    val[j * SUBLANES:(j + 1) * SUBLANES, i * dk:(i + 1) * dk])

    lbl = lbl_ref[...]
    lbe = jnp.exp(lbl - jnp.max(lbl, axis=0, keepdims=True))
    lbs = lbe / jnp.sum(lbe, axis=0, keepdims=True)
    lb_all = jnp.sum(lbs[:layer + 1], axis=0, keepdims=True)
    for t, cols in enumerate(col_tiles):
        lb = lb_all[:, t * MXU_TILE:(t + 1) * MXU_TILE]
        hq = proj(3, t)
        q_scr[:, cols] = hq * _sigmoid(hq)
        sf = _sigmoid(proj(4, t))
        store_blocks(k_scr, t, (1.0 - lb) * (1.0 - sf))
        store_blocks(a_scr, t, jnp.log2(lb + (1.0 - lb) * sf))
        store_blocks(v_scr, t, proj(5, t))
    for sh in (1, 2, 4):
        for j in range(nblk):
            data = pl.ds(PITCH * j + SUBLANES, SUBLANES)
            a_scr[:, data, :] = a_scr[:, data, :] + a_scr[:, pl.ds(PITCH * j + SUBLANES - sh, SUBLANES), :]

    stage1 = {it: _hgrn2_scores(a_scr, q_scr, k_scr, v_scr, it[0] * chunk, it[1], chunk) for it in items}

    for t, cols in enumerate(col_tiles):
        p = proj(1, t) * proj(2, t)
        for i in range(heads_per_tile):
            p_scr[t * heads_per_tile + i, pl.ds(SUBLANES, tm), :] = p[:, i * dk:(i + 1) * dk]
    for t, cols in enumerate(col_tiles):
        conv = []
        for i in range(heads_per_tile):
            hd = t * heads_per_tile + i
            hc = pl.ds(hd * dk, dk)
            conv.append(convb_ref[:, hc]
                        + convw_ref[0:1, hc] * p_scr[hd, pl.ds(SUBLANES - 2, tm), :]
                        + convw_ref[1:2, hc] * p_scr[hd, pl.ds(SUBLANES - 1, tm), :]
                        + convw_ref[2:3, hc] * p_scr[hd, pl.ds(SUBLANES, tm), :])
        yain_scr[:, cols] = (proj(0, t) * jnp.concatenate(conv, axis=1)).astype(BF16)
    p_scr[:, pl.ds(0, SUBLANES), :] = p_scr[:, pl.ds(tm, SUBLANES), :]
    for cols in col_tiles:
        ya_scr[:, cols] = jnp.dot(yain_scr[...], wconv_ref[:, cols], preferred_element_type=F32)

    for ci, hd in items:
        o_scr[pl.ds(ci * chunk, chunk), pl.ds(hd * dk, dk)] = _hgrn2_leaf(
            a_scr, q_scr, k_scr, v_scr, ci * chunk, hd, chunk)
    chunk_state = {}
    for hd in range(N_HEADS):
        st = state_scr[hd]
        for ci in range(n_chunks):
            chunk_state[ci, hd] = st.astype(BF16)
            _, _, d_state, decay = stage1[ci, hd]
            st = st * decay + d_state
        state_scr[hd] = st

    for t, cols in enumerate(col_tiles):
        hg = proj(6, t)
        og_scr[:, cols] = hg * _sigmoid(hg)

    for ci, hd in items:
        scores, q_state, _, _ = stage1[ci, hd]
        rows, cols = pl.ds(ci * chunk, chunk), pl.ds(hd * dk, dk)
        o_scr[rows, cols] = o_scr[rows, cols] + _hgrn2_output(
            scores, q_state, chunk_state[ci, hd], v_scr, ci * chunk, hd, chunk)

    for t, cols in enumerate(col_tiles):
        ga_scr[:, cols] = _sigmoid(proj(7, t))
        gb_scr[:, cols] = _sigmoid(proj(8, t))

    for hd in range(N_HEADS):
        cols = pl.ds(hd * dk, dk)
        o = o_scr[:, cols]
        ob_scr[:, cols] = (o * _rms_scale(o) * hgn_ref[:, cols] * og_scr[:, cols]).astype(BF16)
    for cols in col_tiles:
        yb = jnp.dot(ob_scr[...], whg_ref[:, cols], preferred_element_type=F32)
        m_scr[:, cols] = (ga_scr[:, cols] * ya_scr[:, cols] + gb_scr[:, cols] * yb).astype(BF16)
    ssq = jnp.zeros((tm, 1), F32)
    for cols in col_tiles:
        y = jnp.dot(m_scr[...], wout_ref[:, cols], preferred_element_type=F32)
        y_scr[:, cols] = y
        ssq = ssq + jnp.sum(y * y, axis=-1, keepdims=True)
    o_ref[0] = x + (gate * gpost_ref[...]) * (y_scr[...] * lax.rsqrt(ssq * (1.0 / d) + EPS))


def _mixer(x, ada, g_pre, g_post, w_mix_in, conv_w, conv_b, w_conv_out, hg_norm_g, lb_logits,
           w_hg_out, w_mix_out, layer):
    bsz, seq, d = x.shape
    tm = min(MIX_TM, seq)
    chunk = min(MIX_CHUNK, tm)
    dk = d // N_HEADS
    assert seq % tm == 0 and tm % chunk == 0 and dk == LANES and d % MXU_TILE == 0
    kern = functools.partial(_mixer_kernel, layer=layer, chunk=chunk)
    tile = pltpu.VMEM((tm, d), F32)
    tile_bf16 = pltpu.VMEM((tm, d), BF16)
    padded = pltpu.VMEM((N_HEADS, 2 * tm, dk), F32)
    return pl.pallas_call(
        kern,
        grid=(bsz, seq // tm),
        in_specs=[
            pl.BlockSpec((1, tm, d), lambda b, s: (b, s, 0)),
            pl.BlockSpec((1, ADA_SLOTS, d), lambda b, s: (b, 0, 0)),
            pl.BlockSpec((1, d), lambda b, s: (0, 0)),
            pl.BlockSpec((1, d), lambda b, s: (0, 0)),
            _resident(w_mix_in.shape),
            pl.BlockSpec((CONV_K, d), lambda b, s: (0, 0)),
            pl.BlockSpec((1, d), lambda b, s: (0, 0)),
            _resident(w_conv_out.shape),
            pl.BlockSpec((1, d), lambda b, s: (0, 0)),
            pl.BlockSpec(lb_logits.shape, lambda b, s: (0, 0)),
            _resident(w_hg_out.shape),
            _resident(w_mix_out.shape),
        ],
        out_specs=pl.BlockSpec((1, tm, d), lambda b, s: (b, s, 0)),
        out_shape=jax.ShapeDtypeStruct(x.shape, F32),
        scratch_shapes=[
            pltpu.VMEM((N_HEADS, dk, dk), F32),
            tile, padded, padded, padded,
            pltpu.VMEM((N_HEADS, tm + SUBLANES, dk), F32),
            tile, tile, tile, tile,
            tile, tile,
            tile_bf16, tile_bf16, tile_bf16, tile_bf16,
        ],
        compiler_params=pltpu.CompilerParams(
            dimension_semantics=("arbitrary", "arbitrary"), vmem_limit_bytes=VMEM_LIMIT),
        name="mixer",
    )(x, ada, g_pre, g_post, w_mix_in, conv_w, conv_b, w_conv_out, hg_norm_g, lb_logits,
      w_hg_out, w_mix_out)


def kernel(x, c, w_ada, b_ada, norm_gains, w_ffn1_in, w_ffn1_out, w_mix_in, conv_w, conv_b,
           w_conv_out, hg_norm_g, lb_logits, w_hg_out, w_mix_out, w_ffn2_in, w_ffn2_out):
    depth = w_ada.shape[0]
    bsz, _, d = x.shape
    for l in range(depth):
        ada = _ada(c, w_ada[l], b_ada[l]).reshape(bsz, ADA_SLOTS, d)
        ng = norm_gains[l]
        gain = lambda i: ng[i:i + 1]
        x = _ffn(x, ada, gain(0), gain(1), w_ffn1_in[l].astype(BF16), _weight(w_ffn1_out[l]), 0)
        x = _mixer(x, ada, gain(2), gain(3), _weight(w_mix_in[l]), conv_w[l], conv_b[l:l + 1],
                   _weight(w_conv_out[l]), hg_norm_g[l:l + 1], lb_logits,
                   _weight(w_hg_out[l]), _weight(w_mix_out[l]), l)
        x = _ffn(x, ada, gain(4), gain(5), w_ffn2_in[l].astype(BF16), _weight(w_ffn2_out[l]), 2)
    return x
```

```python
import functools
import math

import jax
import jax.numpy as jnp
from jax import lax
from jax.experimental import pallas as pl
from jax.experimental.pallas import tpu as pltpu

EPS = 1e-6
FFN_RES_W = 0.5
N_HEADS = 8
ADA_SLOTS = 9
CONV_K = 3
N_MIX_GROUPS = 9
NEG_LOG2E = -math.log2(math.e)

F32 = jnp.float32
BF16 = jnp.bfloat16

LANES = 128
SUBLANES = 8
MXU_TILE = 256

FFN_TM = 1024
FFN_ROWS = 256
FFN_TF = 256
MIX_TM = 256
MIX_CHUNK = 128
VMEM_LIMIT = 56 * 1024 * 1024


def _sigmoid(x):
    return 1.0 / (1.0 + jnp.exp2(x * NEG_LOG2E))


def _rms_scale(x):
    return lax.rsqrt(jnp.mean(x * x, axis=-1, keepdims=True) + EPS)


def _resident(shape):
    nd = len(shape)
    return pl.BlockSpec(shape, lambda *_: (0,) * nd, pipeline_mode=pl.Buffered(1))


def _weight(w):
    return jnp.pad(w.astype(BF16), ((0, 0), (0, LANES)))


def _ada_kernel(c_ref, w_ref, b_ref, o_ref):
    c = c_ref[...]
    ca = (c * _sigmoid(c)).astype(BF16)
    o_ref[...] = jnp.dot(ca, w_ref[...].astype(BF16), preferred_element_type=F32) + b_ref[...]


def _ada(c, w, b):
    bsz, d = c.shape
    n = w.shape[1]
    tn = d
    return pl.pallas_call(
        _ada_kernel,
        grid=(n // tn,),
        in_specs=[
            pl.BlockSpec((bsz, d), lambda j: (0, 0)),
            pl.BlockSpec((d, tn), lambda j: (0, j)),
            pl.BlockSpec((1, tn), lambda j: (0, j)),
        ],
        out_specs=pl.BlockSpec((bsz, tn), lambda j: (0, j)),
        out_shape=jax.ShapeDtypeStruct((bsz, n), F32),
        compiler_params=pltpu.CompilerParams(dimension_semantics=("arbitrary",)),
        name="ada_proj",
    )(c, w, b.reshape(1, n))


def _ffn_kernel(x_ref, ada_ref, gpre_ref, gpost_ref, win_ref, wout_ref, o_ref, g_scr, y_scr,
                *, slot, d_ff, tf):
    tm, d = y_scr.shape
    shift = ada_ref[0, 3 * slot:3 * slot + 1, :]
    scale = ada_ref[0, 3 * slot + 1:3 * slot + 2, :]
    gate = ada_ref[0, 3 * slot + 2:3 * slot + 3, :]
    groups = [pl.ds(i, FFN_ROWS) for i in range(0, tm, FFN_ROWS)]
    hbs = []
    for rows in groups:
        x = x_ref[0, rows, :]
        h = x * _rms_scale(x) * gpre_ref[...]
        hbs.append((h * (1.0 + scale) + shift).astype(BF16))
    for rows, hb in zip(groups, hbs):
        for j in range(d_ff // tf):
            a = jnp.dot(hb, win_ref[:, j * tf:(j + 1) * tf], preferred_element_type=F32)
            b = jnp.dot(hb, win_ref[:, d_ff + j * tf:d_ff + (j + 1) * tf], preferred_element_type=F32)
            g_scr[rows, j * tf:(j + 1) * tf] = (a * _sigmoid(a) * b).astype(BF16)
        ssq = jnp.zeros((FFN_ROWS, 1), F32)
        for j in range(0, d, MXU_TILE):
            y = jnp.dot(g_scr[rows, :], wout_ref[:, j:j + MXU_TILE], preferred_element_type=F32)
            y_scr[rows, j:j + MXU_TILE] = y
            ssq = ssq + jnp.sum(y * y, axis=-1, keepdims=True)
        o_ref[0, rows, :] = x_ref[0, rows, :] + (FFN_RES_W * gate * gpost_ref[...]) * (
            y_scr[rows, :] * lax.rsqrt(ssq * (1.0 / d) + EPS))


def _ffn(x, ada, g_pre, g_post, w_in, w_out, slot):
    bsz, seq, d = x.shape
    d_ff = w_out.shape[0]
    tm = min(FFN_TM, seq)
    tf = FFN_TF
    assert seq % tm == 0 and tm % FFN_ROWS == 0 and d_ff % tf == 0 and d % MXU_TILE == 0
    kern = functools.partial(_ffn_kernel, slot=slot, d_ff=d_ff, tf=tf)
    return pl.pallas_call(
        kern,
        grid=(bsz, seq // tm),
        in_specs=[
            pl.BlockSpec((1, tm, d), lambda b, s: (b, s, 0)),
            pl.BlockSpec((1, ADA_SLOTS, d), lambda b, s: (b, 0, 0)),
            pl.BlockSpec((1, d), lambda b, s: (0, 0)),
            pl.BlockSpec((1, d), lambda b, s: (0, 0)),
            _resident(w_in.shape),
            _resident(w_out.shape),
        ],
        out_specs=pl.BlockSpec((1, tm, d), lambda b, s: (b, s, 0)),
        out_shape=jax.ShapeDtypeStruct(x.shape, F32),
        scratch_shapes=[pltpu.VMEM((tm, d_ff), BF16), pltpu.VMEM((tm, d), F32)],
        compiler_params=pltpu.CompilerParams(
            dimension_semantics=("parallel", "parallel"), vmem_limit_bytes=VMEM_LIMIT),
        name=f"ffn{slot}",
    )(x, ada, g_pre, g_post, w_in, w_out)


_NT = (((1,), (1,)), ((), ()))
_TN = (((0,), (0,)), ((), ()))
PITCH = 2 * SUBLANES


def _block_rows(r0, c):
    return [2 * (r0 + SUBLANES * j) + SUBLANES for j in range(c // SUBLANES)]


def _level_split(nblk, nb):
    odd = [j for j in range(nblk) if (j // nb) % 2 == 1]
    even = [j for j in range(nblk) if (j // nb) % 2 == 0]
    return odd, even


def _hgrn2_scores(a_ref, q_ref, k_ref, v_ref, r0, hd, c):
    nblk = c // SUBLANES
    dk = q_ref.shape[-1] // N_HEADS
    base = _block_rows(r0, c)
    qs = [q_ref[pl.ds(r0 + SUBLANES * j, SUBLANES), pl.ds(hd * dk, dk)] for j in range(nblk)]
    ks = [k_ref[hd, pl.ds(base[j], SUBLANES), :] for j in range(nblk)]
    vs = [v_ref[hd, pl.ds(base[j], SUBLANES), :] for j in range(nblk)]
    acc = [a_ref[hd, pl.ds(base[j], SUBLANES), :] for j in range(nblk)]
    tot = [t[SUBLANES - 1:SUBLANES] for t in acc]
    half = c // 2
    ti = lax.broadcasted_iota(jnp.int32, (half, half), 0)
    si = lax.broadcasted_iota(jnp.int32, (half, half), 1)
    scores = []
    b = SUBLANES
    while b < c:
        nb = b // SUBLANES
        odd, even = _level_split(nblk, nb)
        qc = jnp.concatenate([qs[j] * jnp.exp2(acc[j]) for j in odd], axis=0).astype(BF16)
        kc = jnp.concatenate([ks[j] * jnp.exp2(tot[j // nb] - acc[j]) for j in even], axis=0).astype(BF16)
        sc = lax.dot_general(qc, kc, _NT, preferred_element_type=F32)
        shift = b.bit_length() - 1
        same = lax.shift_right_logical(ti, shift) == lax.shift_right_logical(si, shift)
        scores.append(jnp.where(same, sc, 0.0).astype(BF16))
        acc = [acc[j] + tot[j // nb - 1] if (j // nb) % 2 == 1 else acc[j] for j in range(nblk)]
        tot = [tot[2 * m] + tot[2 * m + 1] for m in range(len(tot) // 2)]
        b *= 2
    a_c = jnp.concatenate(acc, axis=0)
    t_c = tot[0]
    q_state = (jnp.concatenate(qs, axis=0) * jnp.exp2(a_c)).astype(BF16)
    k_state = (jnp.concatenate(ks, axis=0) * jnp.exp2(t_c - a_c)).astype(BF16)
    v_all = jnp.concatenate(vs, axis=0).astype(BF16)
    d_state = lax.dot_general(v_all, k_state, _TN, preferred_element_type=F32)
    return scores, q_state, d_state, jnp.exp2(t_c)


def _hgrn2_leaf(a_ref, q_ref, k_ref, v_ref, r0, hd, c):
    dk = q_ref.shape[-1] // N_HEADS
    o = []
    for j, row in enumerate(_block_rows(r0, c)):
        q = q_ref[pl.ds(r0 + SUBLANES * j, SUBLANES), pl.ds(hd * dk, dk)]
        data = pl.ds(row, SUBLANES)
        acc = a_ref[hd, data, :]
        oj = jnp.sum(q * k_ref[hd, data, :], axis=-1, keepdims=True) * v_ref[hd, data, :]
        for dlt in range(1, SUBLANES):
            sh = pl.ds(row - dlt, SUBLANES)
            w = jnp.exp2(acc - a_ref[hd, sh, :])
            s = jnp.sum(q * w * k_ref[hd, sh, :], axis=-1, keepdims=True)
            oj = oj + s * v_ref[hd, sh, :]
        o.append(oj)
    return jnp.concatenate(o, axis=0)


def _hgrn2_output(scores, q_state, st_bf16, v_ref, r0, hd, c):
    nblk = c // SUBLANES
    base = _block_rows(r0, c)
    vs = [v_ref[hd, pl.ds(base[j], SUBLANES), :] for j in range(nblk)]
    o_inter = lax.dot_general(q_state, st_bf16, _NT, preferred_element_type=F32)
    o = [o_inter[j * SUBLANES:(j + 1) * SUBLANES] for j in range(nblk)]
    b = SUBLANES
    for sc in scores:
        odd, even = _level_split(nblk, b // SUBLANES)
        vc = jnp.concatenate([vs[j] for j in even], axis=0).astype(BF16)
        contrib = jnp.dot(sc, vc, preferred_element_type=F32)
        for i, j in enumerate(odd):
            o[j] = o[j] + contrib[i * SUBLANES:(i + 1) * SUBLANES]
        b *= 2
    return jnp.concatenate(o, axis=0)


def _mixer_kernel(x_ref, ada_ref, gpre_ref, gpost_ref, wmix_ref, convw_ref, convb_ref, wconv_ref,
                  hgn_ref, lbl_ref, whg_ref, wout_ref, o_ref,
                  state_scr, q_scr, k_scr, v_scr, a_scr, p_scr, og_scr, ga_scr, gb_scr, ya_scr, o_scr,
                  y_scr, hb_scr, yain_scr, ob_scr, m_scr, *, layer, chunk):
    tm, d = q_scr.shape
    dk = d // N_HEADS
    nblk = tm // SUBLANES
    n_chunks = tm // chunk
    items = [(ci, hd) for ci in range(n_chunks) for hd in range(N_HEADS)]
    col_tiles = [pl.ds(j, MXU_TILE) for j in range(0, d, MXU_TILE)]
    heads_per_tile = MXU_TILE // dk
    s_idx = pl.program_id(1)

    @pl.when(s_idx == 0)
    def _():
        state_scr[...] = jnp.zeros_like(state_scr)
        zeros = jnp.zeros((N_HEADS, SUBLANES, dk), F32)
        p_scr[:, pl.ds(0, SUBLANES), :] = zeros
        for j in range(nblk):
            pad = pl.ds(PITCH * j, SUBLANES)
            a_scr[:, pad, :] = zeros
            k_scr[:, pad, :] = zeros
            v_scr[:, pad, :] = zeros

    x = x_ref[0]
    shift = ada_ref[0, 3:4, :]
    scale = ada_ref[0, 4:5, :]
    gate = ada_ref[0, 5:6, :]
    h = x * _rms_scale(x) * gpre_ref[...]
    hb_scr[...] = (h * (1.0 + scale) + shift).astype(BF16)

    def proj(g, t):
        cols = pl.ds(g * d + t * MXU_TILE, MXU_TILE)
        return jnp.dot(hb_scr[...], wmix_ref[:, cols], preferred_element_type=F32)

    def store_blocks(ref, t, val):
        for i in range(heads_per_tile):
            hd = t * heads_per_tile + i
            for j in range(nblk):
                ref[hd, pl.ds(PITCH * j + SUBLANES, SUBLANES), :] = (
                    val[j * SUBLANES:(j + 1) * SUBLANES, i * dk:(i + 1) * dk])

    lbl = lbl_ref[...]
    lbe = jnp.exp(lbl - jnp.max(lbl, axis=0, keepdims=True))
    lbs = lbe / jnp.sum(lbe, axis=0, keepdims=True)
    lb_all = jnp.sum(lbs[:layer + 1], axis=0, keepdims=True)
    for t, cols in enumerate(col_tiles):
        lb = lb_all[:, t * MXU_TILE:(t + 1) * MXU_TILE]
        hq = proj(3, t)
        q_scr[:, cols] = hq * _sigmoid(hq)
        sf = _sigmoid(proj(4, t))
        store_blocks(k_scr, t, (1.0 - lb) * (1.0 - sf))
        store_blocks(a_scr, t, jnp.log2(lb + (1.0 - lb) * sf))
        store_blocks(v_scr, t, proj(5, t))
    for sh in (1, 2, 4):
        for j in range(nblk):
            data = pl.ds(PITCH * j + SUBLANES, SUBLANES)
            a_scr[:, data, :] = a_scr[:, data, :] + a_scr[:, pl.ds(PITCH * j + SUBLANES - sh, SUBLANES), :]

    stage1 = {it: _hgrn2_scores(a_scr, q_scr, k_scr, v_scr, it[0] * chunk, it[1], chunk) for it in items}

    for t, cols in enumerate(col_tiles):
        p = proj(1, t) * proj(2, t)
        for i in range(heads_per_tile):
            p_scr[t * heads_per_tile + i, pl.ds(SUBLANES, tm), :] = p[:, i * dk:(i + 1) * dk]
    for t, cols in enumerate(col_tiles):
        conv = []
        for i in range(heads_per_tile):
            hd = t * heads_per_tile + i
            hc = pl.ds(hd * dk, dk)
            conv.append(convb_ref[:, hc]
                        + convw_ref[0:1, hc] * p_scr[hd, pl.ds(SUBLANES - 2, tm), :]
                        + convw_ref[1:2, hc] * p_scr[hd, pl.ds(SUBLANES - 1, tm), :]
                        + convw_ref[2:3, hc] * p_scr[hd, pl.ds(SUBLANES, tm), :])
        yain_scr[:, cols] = (proj(0, t) * jnp.concatenate(conv, axis=1)).astype(BF16)
    p_scr[:, pl.ds(0, SUBLANES), :] = p_scr[:, pl.ds(tm, SUBLANES), :]
    for cols in col_tiles:
        ya_scr[:, cols] = jnp.dot(yain_scr[...], wconv_ref[:, cols], preferred_element_type=F32)

    for ci, hd in items:
        o_scr[pl.ds(ci * chunk, chunk), pl.ds(hd * dk, dk)] = _hgrn2_leaf(
            a_scr, q_scr, k_scr, v_scr, ci * chunk, hd, chunk)
    chunk_state = {}
    for hd in range(N_HEADS):
        st = state_scr[hd]
        for ci in range(n_chunks):
            chunk_state[ci, hd] = st.astype(BF16)
            _, _, d_state, decay = stage1[ci, hd]
            st = st * decay + d_state
        state_scr[hd] = st

    for t, cols in enumerate(col_tiles):
        hg = proj(6, t)
        og_scr[:, cols] = hg * _sigmoid(hg)

    for ci, hd in items:
        scores, q_state, _, _ = stage1[ci, hd]
        rows, cols = pl.ds(ci * chunk, chunk), pl.ds(hd * dk, dk)
        o_scr[rows, cols] = o_scr[rows, cols] + _hgrn2_output(
            scores, q_state, chunk_state[ci, hd], v_scr, ci * chunk, hd, chunk)

    for t, cols in enumerate(col_tiles):
        ga_scr[:, cols] = _sigmoid(proj(7, t))
        gb_scr[:, cols] = _sigmoid(proj(8, t))

    for hd in range(N_HEADS):
        cols = pl.ds(hd * dk, dk)
        o = o_scr[:, cols]
        ob_scr[:, cols] = (o * _rms_scale(o) * hgn_ref[:, cols] * og_scr[:, cols]).astype(BF16)
    for cols in col_tiles:
        yb = jnp.dot(ob_scr[...], whg_ref[:, cols], preferred_element_type=F32)
        m_scr[:, cols] = (ga_scr[:, cols] * ya_scr[:, cols] + gb_scr[:, cols] * yb).astype(BF16)
    ssq = jnp.zeros((tm, 1), F32)
    for cols in col_tiles:
        y = jnp.dot(m_scr[...], wout_ref[:, cols], preferred_element_type=F32)
        y_scr[:, cols] = y
        ssq = ssq + jnp.sum(y * y, axis=-1, keepdims=True)
    o_ref[0] = x + (gate * gpost_ref[...]) * (y_scr[...] * lax.rsqrt(ssq * (1.0 / d) + EPS))


def _mixer(x, ada, g_pre, g_post, w_mix_in, conv_w, conv_b, w_conv_out, hg_norm_g, lb_logits,
           w_hg_out, w_mix_out, layer):
    bsz, seq, d = x.shape
    tm = min(MIX_TM, seq)
    chunk = min(MIX_CHUNK, tm)
    dk = d // N_HEADS
    assert seq % tm == 0 and tm % chunk == 0 and dk == LANES and d % MXU_TILE == 0
    kern = functools.partial(_mixer_kernel, layer=layer, chunk=chunk)
    tile = pltpu.VMEM((tm, d), F32)
    tile_bf16 = pltpu.VMEM((tm, d), BF16)
    padded = pltpu.VMEM((N_HEADS, 2 * tm, dk), F32)
    return pl.pallas_call(
        kern,
        grid=(bsz, seq // tm),
        in_specs=[
            pl.BlockSpec((1, tm, d), lambda b, s: (b, s, 0)),
            pl.BlockSpec((1, ADA_SLOTS, d), lambda b, s: (b, 0, 0)),
            pl.BlockSpec((1, d), lambda b, s: (0, 0)),
            pl.BlockSpec((1, d), lambda b, s: (0, 0)),
            _resident(w_mix_in.shape),
            pl.BlockSpec((CONV_K, d), lambda b, s: (0, 0)),
            pl.BlockSpec((1, d), lambda b, s: (0, 0)),
            _resident(w_conv_out.shape),
            pl.BlockSpec((1, d), lambda b, s: (0, 0)),
            pl.BlockSpec(lb_logits.shape, lambda b, s: (0, 0)),
            _resident(w_hg_out.shape),
            _resident(w_mix_out.shape),
        ],
        out_specs=pl.BlockSpec((1, tm, d), lambda b, s: (b, s, 0)),
        out_shape=jax.ShapeDtypeStruct(x.shape, F32),
        scratch_shapes=[
            pltpu.VMEM((N_HEADS, dk, dk), F32),
            tile, padded, padded, padded,
            pltpu.VMEM((N_HEADS, tm + SUBLANES, dk), F32),
            tile, tile, tile, tile,
            tile, tile,
            tile_bf16, tile_bf16, tile_bf16, tile_bf16,
        ],
        compiler_params=pltpu.CompilerParams(
            dimension_semantics=("arbitrary", "arbitrary"), vmem_limit_bytes=VMEM_LIMIT),
        name="mixer",
    )(x, ada, g_pre, g_post, w_mix_in, conv_w, conv_b, w_conv_out, hg_norm_g, lb_logits,
      w_hg_out, w_mix_out)


def kernel(x, c, w_ada, b_ada, norm_gains, w_ffn1_in, w_ffn1_out, w_mix_in, conv_w, conv_b,
           w_conv_out, hg_norm_g, lb_logits, w_hg_out, w_mix_out, w_ffn2_in, w_ffn2_out):
    depth = w_ada.shape[0]
    bsz, _, d = x.shape
    for l in range(depth):
        ada = _ada(c, w_ada[l], b_ada[l]).reshape(bsz, ADA_SLOTS, d)
        ng = norm_gains[l]
        gain = lambda i: ng[i:i + 1]
        x = _ffn(x, ada, gain(0), gain(1), w_ffn1_in[l].astype(BF16), _weight(w_ffn1_out[l]), 0)
        x = _mixer(x, ada, gain(2), gain(3), _weight(w_mix_in[l]), conv_w[l], conv_b[l:l + 1],
                   _weight(w_conv_out[l]), hg_norm_g[l:l + 1], lb_logits,
                   _weight(w_hg_out[l]), _weight(w_mix_out[l]), l)
        x = _ffn(x, ada, gain(4), gain(5), w_ffn2_in[l].astype(BF16), _weight(w_ffn2_out[l]), 2)
    return x
```

```python
import functools
import math

import jax
import jax.numpy as jnp
from jax import lax
from jax.experimental import pallas as pl
from jax.experimental.pallas import tpu as pltpu

EPS = 1e-6
FFN_RES_W = 0.5
N_HEADS = 8
ADA_SLOTS = 9
CONV_K = 3
NEG_LOG2E = -math.log2(math.e)

F32 = jnp.float32
BF16 = jnp.bfloat16

LANES = 128
SUBLANES = 8
MXU_TILE = 256

FFN_TM = 1024
FFN_ROWS = 512
FFN_TF = 256
MIX_TM = 256
MIX_CHUNK = 128
VMEM_LIMIT = 56 * 1024 * 1024


def _sigmoid(x):
    return 1.0 / (1.0 + jnp.exp2(x * NEG_LOG2E))


def _rms_scale(x):
    return lax.rsqrt(jnp.mean(x * x, axis=-1, keepdims=True) + EPS)


def _resident(shape):
    nd = len(shape)
    return pl.BlockSpec(shape, lambda *_: (0,) * nd, pipeline_mode=pl.Buffered(1))


def _weight(w):
    return jnp.pad(w.astype(BF16), ((0, 0), (0, LANES)))


def _ada_kernel(c_ref, w_ref, b_ref, o_ref):
    c = c_ref[...]
    ca = (c * _sigmoid(c)).astype(BF16)
    o_ref[...] = jnp.dot(ca, w_ref[...].astype(BF16), preferred_element_type=F32) + b_ref[...]


def _ada(c, w, b):
    bsz, d = c.shape
    n = w.shape[1]
    tn = d
    return pl.pallas_call(
        _ada_kernel,
        grid=(n // tn,),
        in_specs=[
            pl.BlockSpec((bsz, d), lambda j: (0, 0)),
            pl.BlockSpec((d, tn), lambda j: (0, j)),
            pl.BlockSpec((1, tn), lambda j: (0, j)),
        ],
        out_specs=pl.BlockSpec((bsz, tn), lambda j: (0, j)),
        out_shape=jax.ShapeDtypeStruct((bsz, n), F32),
        compiler_params=pltpu.CompilerParams(dimension_semantics=("arbitrary",)),
        name="ada_proj",
    )(c, w, b.reshape(1, n))


def _ffn_kernel(x_ref, ada_ref, gpre_ref, gpost_ref, win_ref, wout_ref, o_ref, g_scr, y_scr,
                *, slot, d_ff, tf):
    tm, d = y_scr.shape
    shift = ada_ref[0, 3 * slot:3 * slot + 1, :]
    scale = ada_ref[0, 3 * slot + 1:3 * slot + 2, :]
    gate = ada_ref[0, 3 * slot + 2:3 * slot + 3, :]
    groups = [pl.ds(i, FFN_ROWS) for i in range(0, tm, FFN_ROWS)]
    hbs = []
    for rows in groups:
        x = x_ref[0, rows, :]
        h = x * _rms_scale(x) * gpre_ref[...]
        hbs.append((h * (1.0 + scale) + shift).astype(BF16))
    for rows, hb in zip(groups, hbs):
        for j in range(d_ff // tf):
            a = jnp.dot(hb, win_ref[:, j * tf:(j + 1) * tf], preferred_element_type=F32)
            b = jnp.dot(hb, win_ref[:, d_ff + j * tf:d_ff + (j + 1) * tf], preferred_element_type=F32)
            g_scr[rows, j * tf:(j + 1) * tf] = (a * _sigmoid(a) * b).astype(BF16)
        ssq = jnp.zeros((FFN_ROWS, 1), F32)
        for j in range(0, d, MXU_TILE):
            y = jnp.dot(g_scr[rows, :], wout_ref[:, j:j + MXU_TILE], preferred_element_type=F32)
            y_scr[rows, j:j + MXU_TILE] = y
            ssq = ssq + jnp.sum(y * y, axis=-1, keepdims=True)
        o_ref[0, rows, :] = x_ref[0, rows, :] + (FFN_RES_W * gate * gpost_ref[...]) * (
            y_scr[rows, :] * lax.rsqrt(ssq * (1.0 / d) + EPS))


def _ffn(x, ada, g_pre, g_post, w_in, w_out, slot):
    bsz, seq, d = x.shape
    d_ff = w_out.shape[0]
    tm = min(FFN_TM, seq)
    tf = FFN_TF
    assert seq % tm == 0 and tm % FFN_ROWS == 0 and d_ff % tf == 0 and d % MXU_TILE == 0
    kern = functools.partial(_ffn_kernel, slot=slot, d_ff=d_ff, tf=tf)
    return pl.pallas_call(
        kern,
        grid=(bsz, seq // tm),
        in_specs=[
            pl.BlockSpec((1, tm, d), lambda b, s: (b, s, 0)),
            pl.BlockSpec((1, ADA_SLOTS, d), lambda b, s: (b, 0, 0)),
            pl.BlockSpec((1, d), lambda b, s: (0, 0)),
            pl.BlockSpec((1, d), lambda b, s: (0, 0)),
            _resident(w_in.shape),
            _resident(w_out.shape),
        ],
        out_specs=pl.BlockSpec((1, tm, d), lambda b, s: (b, s, 0)),
        out_shape=jax.ShapeDtypeStruct(x.shape, F32),
        scratch_shapes=[pltpu.VMEM((tm, d_ff), BF16), pltpu.VMEM((tm, d), F32)],
        compiler_params=pltpu.CompilerParams(
            dimension_semantics=("parallel", "parallel"), vmem_limit_bytes=VMEM_LIMIT),
        name=f"ffn{slot}",
    )(x, ada, g_pre, g_post, w_in, w_out)


_NT = (((1,), (1,)), ((), ()))
_TN = (((0,), (0,)), ((), ()))
PITCH = 2 * SUBLANES


def _block_rows(r0, c):
    return [2 * (r0 + SUBLANES * j) + SUBLANES for j in range(c // SUBLANES)]


def _level_split(nblk, nb):
    odd = [j for j in range(nblk) if (j // nb) % 2 == 1]
    even = [j for j in range(nblk) if (j // nb) % 2 == 0]
    return odd, even


def _hgrn2_scores(a_ref, q_ref, k_ref, v_ref, r0, hd, c):
    nblk = c // SUBLANES
    dk = q_ref.shape[-1] // N_HEADS
    base = _block_rows(r0, c)
    qs = [q_ref[pl.ds(r0 + SUBLANES * j, SUBLANES), pl.ds(hd * dk, dk)] for j in range(nblk)]
    ks = [k_ref[hd, pl.ds(base[j], SUBLANES), :] for j in range(nblk)]
    vs = [v_ref[hd, pl.ds(base[j], SUBLANES), :] for j in range(nblk)]
    acc = [a_ref[hd, pl.ds(base[j], SUBLANES), :] for j in range(nblk)]
    tot = [t[SUBLANES - 1:SUBLANES] for t in acc]
    half = c // 2
    ti = lax.broadcasted_iota(jnp.int32, (half, half), 0)
    si = lax.broadcasted_iota(jnp.int32, (half, half), 1)
    scores = []
    b = SUBLANES
    while b < c:
        nb = b // SUBLANES
        odd, even = _level_split(nblk, nb)
        qc = jnp.concatenate([qs[j] * jnp.exp2(acc[j]) for j in odd], axis=0).astype(BF16)
        kc = jnp.concatenate([ks[j] * jnp.exp2(tot[j // nb] - acc[j]) for j in even], axis=0).astype(BF16)
        sc = lax.dot_general(qc, kc, _NT, preferred_element_type=F32)
        shift = b.bit_length() - 1
        same = lax.shift_right_logical(ti, shift) == lax.shift_right_logical(si, shift)
        scores.append(jnp.where(same, sc, 0.0).astype(BF16))
        acc = [acc[j] + tot[j // nb - 1] if (j // nb) % 2 == 1 else acc[j] for j in range(nblk)]
        tot = [tot[2 * m] + tot[2 * m + 1] for m in range(len(tot) // 2)]
        b *= 2
    a_c = jnp.concatenate(acc, axis=0)
    t_c = tot[0]
    q_state = (jnp.concatenate(qs, axis=0) * jnp.exp2(a_c)).astype(BF16)
    k_state = (jnp.concatenate(ks, axis=0) * jnp.exp2(t_c - a_c)).astype(BF16)
    v_all = jnp.concatenate(vs, axis=0).astype(BF16)
    d_state = lax.dot_general(v_all, k_state, _TN, preferred_element_type=F32)
    return scores, q_state, d_state, jnp.exp2(t_c)


def _hgrn2_leaf(a_ref, q_ref, k_ref, v_ref, r0, hd, c):
    dk = q_ref.shape[-1] // N_HEADS
    o = []
    for j, row in enumerate(_block_rows(r0, c)):
        q = q_ref[pl.ds(r0 + SUBLANES * j, SUBLANES), pl.ds(hd * dk, dk)]
        data = pl.ds(row, SUBLANES)
        acc = a_ref[hd, data, :]
        oj = jnp.sum(q * k_ref[hd, data, :], axis=-1, keepdims=True) * v_ref[hd, data, :]
        for dlt in range(1, SUBLANES):
            sh = pl.ds(row - dlt, SUBLANES)
            w = jnp.exp2(acc - a_ref[hd, sh, :])
            s = jnp.sum(q * w * k_ref[hd, sh, :], axis=-1, keepdims=True)
            oj = oj + s * v_ref[hd, sh, :]
        o.append(oj)
    return jnp.concatenate(o, axis=0)


def _hgrn2_output(scores, q_state, st_bf16, v_ref, r0, hd, c):
    nblk = c // SUBLANES
    base = _block_rows(r0, c)
    vs = [v_ref[hd, pl.ds(base[j], SUBLANES), :] for j in range(nblk)]
    o_inter = lax.dot_general(q_state, st_bf16, _NT, preferred_element_type=F32)
    o = [o_inter[j * SUBLANES:(j + 1) * SUBLANES] for j in range(nblk)]
    b = SUBLANES
    for sc in scores:
        odd, even = _level_split(nblk, b // SUBLANES)
        vc = jnp.concatenate([vs[j] for j in even], axis=0).astype(BF16)
        contrib = jnp.dot(sc, vc, preferred_element_type=F32)
        for i, j in enumerate(odd):
            o[j] = o[j] + contrib[i * SUBLANES:(i + 1) * SUBLANES]
        b *= 2
    return jnp.concatenate(o, axis=0)


def _mixer_kernel(x_ref, ada_ref, gpre_ref, gpost_ref, wmix_ref, convw_ref, convb_ref, wconv_ref,
                  hgn_ref, lbl_ref, whg_ref, wout_ref, o_ref,
                  state_scr, q_scr, k_scr, v_scr, a_scr, p_scr, og_scr, ga_scr, gb_scr, ya_scr, o_scr,
                  y_scr, hb_scr, yain_scr, ob_scr, m_scr, *, layer, chunk):
    tm, d = q_scr.shape
    dk = d // N_HEADS
    nblk = tm // SUBLANES
    n_chunks = tm // chunk
    items = [(ci, hd) for ci in range(n_chunks) for hd in range(N_HEADS)]
    col_tiles = [pl.ds(j, MXU_TILE) for j in range(0, d, MXU_TILE)]
    heads_per_tile = MXU_TILE // dk
    s_idx = pl.program_id(1)

    @pl.when(s_idx == 0)
    def _():
        state_scr[...] = jnp.zeros_like(state_scr)
        zeros = jnp.zeros((N_HEADS, SUBLANES, dk), F32)
        p_scr[:, pl.ds(0, SUBLANES), :] = zeros
        for j in range(nblk):
            pad = pl.ds(PITCH * j, SUBLANES)
            a_scr[:, pad, :] = zeros
            k_scr[:, pad, :] = zeros
            v_scr[:, pad, :] = zeros

    x = x_ref[0]
    shift = ada_ref[0, 3:4, :]
    scale = ada_ref[0, 4:5, :]
    gate = ada_ref[0, 5:6, :]
    h = x * _rms_scale(x) * gpre_ref[...]
    hb_scr[...] = (h * (1.0 + scale) + shift).astype(BF16)

    def proj(g, t):
        cols = pl.ds(g * d + t * MXU_TILE, MXU_TILE)
        return jnp.dot(hb_scr[...], wmix_ref[:, cols], preferred_element_type=F32)

    rowi = lax.broadcasted_iota(jnp.int32, (SUBLANES, 1), 0)

    def store_blocks(ref, t, val, cumulate=False):
        for i in range(heads_per_tile):
            hd = t * heads_per_tile + i
            for j in range(nblk):
                blk = val[j * SUBLANES:(j + 1) * SUBLANES, i * dk:(i + 1) * dk]
                if cumulate:
                    for sh in (1, 2, 4):
                        blk = blk + jnp.where(rowi >= sh, pltpu.roll(blk, sh, 0), 0.0)
                ref[hd, pl.ds(PITCH * j + SUBLANES, SUBLANES), :] = blk

    lbl = lbl_ref[...]
    lbe = jnp.exp(lbl - jnp.max(lbl, axis=0, keepdims=True))
    lbs = lbe / jnp.sum(lbe, axis=0, keepdims=True)
    lb_all = jnp.sum(lbs[:layer + 1], axis=0, keepdims=True)
    for t, cols in enumerate(col_tiles):
        lb = lb_all[:, t * MXU_TILE:(t + 1) * MXU_TILE]
        hq = proj(3, t)
        q_scr[:, cols] = hq * _sigmoid(hq)
        sf = _sigmoid(proj(4, t))
        store_blocks(k_scr, t, (1.0 - lb) * (1.0 - sf))
        store_blocks(a_scr, t, jnp.log2(lb + (1.0 - lb) * sf), cumulate=True)
        store_blocks(v_scr, t, proj(5, t))

    stage1 = {it: _hgrn2_scores(a_scr, q_scr, k_scr, v_scr, it[0] * chunk, it[1], chunk) for it in items}

    for t, cols in enumerate(col_tiles):
        p = proj(1, t) * proj(2, t)
        for i in range(heads_per_tile):
            p_scr[t * heads_per_tile + i, pl.ds(SUBLANES, tm), :] = p[:, i * dk:(i + 1) * dk]
    for t, cols in enumerate(col_tiles):
        conv = []
        for i in range(heads_per_tile):
            hd = t * heads_per_tile + i
            hc = pl.ds(hd * dk, dk)
            conv.append(convb_ref[:, hc]
                        + convw_ref[0:1, hc] * p_scr[hd, pl.ds(SUBLANES - 2, tm), :]
                        + convw_ref[1:2, hc] * p_scr[hd, pl.ds(SUBLANES - 1, tm), :]
                        + convw_ref[2:3, hc] * p_scr[hd, pl.ds(SUBLANES, tm), :])
        yain_scr[:, cols] = (proj(0, t) * jnp.concatenate(conv, axis=1)).astype(BF16)
    p_scr[:, pl.ds(0, SUBLANES), :] = p_scr[:, pl.ds(tm, SUBLANES), :]
    for cols in col_tiles:
        ya_scr[:, cols] = jnp.dot(yain_scr[...], wconv_ref[:, cols], preferred_element_type=F32)

    for ci, hd in items:
        o_scr[pl.ds(ci * chunk, chunk), pl.ds(hd * dk, dk)] = _hgrn2_leaf(
            a_scr, q_scr, k_scr, v_scr, ci * chunk, hd, chunk)
    chunk_state = {}
    for hd in range(N_HEADS):
        st = state_scr[hd]
        for ci in range(n_chunks):
            chunk_state[ci, hd] = st.astype(BF16)
            _, _, d_state, decay = stage1[ci, hd]
            st = st * decay + d_state
        state_scr[hd] = st

    for t, cols in enumerate(col_tiles):
        hg = proj(6, t)
        og_scr[:, cols] = hg * _sigmoid(hg)

    for ci, hd in items:
        scores, q_state, _, _ = stage1[ci, hd]
        rows, cols = pl.ds(ci * chunk, chunk), pl.ds(hd * dk, dk)
        o_scr[rows, cols] = o_scr[rows, cols] + _hgrn2_output(
            scores, q_state, chunk_state[ci, hd], v_scr, ci * chunk, hd, chunk)

    for t, cols in enumerate(col_tiles):
        ga_scr[:, cols] = _sigmoid(proj(7, t))
        gb_scr[:, cols] = _sigmoid(proj(8, t))

    for hd in range(N_HEADS):
        cols = pl.ds(hd * dk, dk)
        o = o_scr[:, cols]
        ob_scr[:, cols] = (o * _rms_scale(o) * hgn_ref[:, cols] * og_scr[:, cols]).astype(BF16)
    for cols in col_tiles:
        yb = jnp.dot(ob_scr[...], whg_ref[:, cols], preferred_element_type=F32)
        m_scr[:, cols] = (ga_scr[:, cols] * ya_scr[:, cols] + gb_scr[:, cols] * yb).astype(BF16)
    ssq = jnp.zeros((tm, 1), F32)
    for cols in col_tiles:
        y = jnp.dot(m_scr[...], wout_ref[:, cols], preferred_element_type=F32)
        y_scr[:, cols] = y
        ssq = ssq + jnp.sum(y * y, axis=-1, keepdims=True)
    o_ref[0] = x + (gate * gpost_ref[...]) * (y_scr[...] * lax.rsqrt(ssq * (1.0 / d) + EPS))


def _mixer(x, ada, g_pre, g_post, w_mix_in, conv_w, conv_b, w_conv_out, hg_norm_g, lb_logits,
           w_hg_out, w_mix_out, layer):
    bsz, seq, d = x.shape
    tm = min(MIX_TM, seq)
    chunk = min(MIX_CHUNK, tm)
    dk = d // N_HEADS
    assert seq % tm == 0 and tm % chunk == 0 and dk == LANES and d % MXU_TILE == 0
    kern = functools.partial(_mixer_kernel, layer=layer, chunk=chunk)
    tile = pltpu.VMEM((tm, d), F32)
    tile_bf16 = pltpu.VMEM((tm, d), BF16)
    padded = pltpu.VMEM((N_HEADS, 2 * tm, dk), F32)
    return pl.pallas_call(
        kern,
        grid=(bsz, seq // tm),
        in_specs=[
            pl.BlockSpec((1, tm, d), lambda b, s: (b, s, 0)),
            pl.BlockSpec((1, ADA_SLOTS, d), lambda b, s: (b, 0, 0)),
            pl.BlockSpec((1, d), lambda b, s: (0, 0)),
            pl.BlockSpec((1, d), lambda b, s: (0, 0)),
            _resident(w_mix_in.shape),
            pl.BlockSpec((CONV_K, d), lambda b, s: (0, 0)),
            pl.BlockSpec((1, d), lambda b, s: (0, 0)),
            _resident(w_conv_out.shape),
            pl.BlockSpec((1, d), lambda b, s: (0, 0)),
            pl.BlockSpec(lb_logits.shape, lambda b, s: (0, 0)),
            _resident(w_hg_out.shape),
            _resident(w_mix_out.shape),
        ],
        out_specs=pl.BlockSpec((1, tm, d), lambda b, s: (b, s, 0)),
        out_shape=jax.ShapeDtypeStruct(x.shape, F32),
        scratch_shapes=[
            pltpu.VMEM((N_HEADS, dk, dk), F32),
            tile, padded, padded, padded,
            pltpu.VMEM((N_HEADS, tm + SUBLANES, dk), F32),
            tile, tile, tile, tile,
            tile, tile,
            tile_bf16, tile_bf16, tile_bf16, tile_bf16,
        ],
        compiler_params=pltpu.CompilerParams(
            dimension_semantics=("arbitrary", "arbitrary"), vmem_limit_bytes=VMEM_LIMIT),
        name="mixer",
    )(x, ada, g_pre, g_post, w_mix_in, conv_w, conv_b, w_conv_out, hg_norm_g, lb_logits,
      w_hg_out, w_mix_out)


def kernel(x, c, w_ada, b_ada, norm_gains, w_ffn1_in, w_ffn1_out, w_mix_in, conv_w, conv_b,
           w_conv_out, hg_norm_g, lb_logits, w_hg_out, w_mix_out, w_ffn2_in, w_ffn2_out):
    depth = w_ada.shape[0]
    bsz, _, d = x.shape
    for l in range(depth):
        ada = _ada(c, w_ada[l], b_ada[l]).reshape(bsz, ADA_SLOTS, d)
        ng = norm_gains[l]
        gain = lambda i: ng[i:i + 1]
        x = _ffn(x, ada, gain(0), gain(1), w_ffn1_in[l].astype(BF16), _weight(w_ffn1_out[l]), 0)
        x = _mixer(x, ada, gain(2), gain(3), _weight(w_mix_in[l]), conv_w[l], conv_b[l:l + 1],
                   _weight(w_conv_out[l]), hg_norm_g[l:l + 1], lb_logits,
                   _weight(w_hg_out[l]), _weight(w_mix_out[l]), l)
        x = _ffn(x, ada, gain(4), gain(5), w_ffn2_in[l].astype(BF16), _weight(w_ffn2_out[l]), 2)
    return x
```

```python
import functools
import math

import jax
import jax.numpy as jnp
from jax import lax
from jax.experimental import pallas as pl
from jax.experimental.pallas import tpu as pltpu

EPS = 1e-6
FFN_RES_W = 0.5
N_HEADS = 8
ADA_SLOTS = 9
CONV_K = 3
NEG_LOG2E = -math.log2(math.e)

F32 = jnp.float32
BF16 = jnp.bfloat16

LANES = 128
SUBLANES = 8
MXU_TILE = 256

FFN_TM = 1024
FFN_ROWS = 512
FFN_TF = 256
MIX_TM = 256
MIX_CHUNK = 128
VMEM_LIMIT = 56 * 1024 * 1024


def _sigmoid(x):
    return 1.0 / (1.0 + jnp.exp2(x * NEG_LOG2E))


def _rms_scale(x):
    return lax.rsqrt(jnp.mean(x * x, axis=-1, keepdims=True) + EPS)


def _resident(shape):
    nd = len(shape)
    return pl.BlockSpec(shape, lambda *_: (0,) * nd, pipeline_mode=pl.Buffered(1))


def _weight(w):
    return jnp.pad(w.astype(BF16), ((0, 0), (0, LANES)))


def _ada_kernel(c_ref, w_ref, b_ref, o_ref):
    c = c_ref[...]
    ca = (c * _sigmoid(c)).astype(BF16)
    o_ref[...] = jnp.dot(ca, w_ref[...].astype(BF16), preferred_element_type=F32) + b_ref[...]


def _ada(c, w, b):
    bsz, d = c.shape
    n = w.shape[1]
    tn = d
    return pl.pallas_call(
        _ada_kernel,
        grid=(n // tn,),
        in_specs=[
            pl.BlockSpec((bsz, d), lambda j: (0, 0)),
            pl.BlockSpec((d, tn), lambda j: (0, j)),
            pl.BlockSpec((1, tn), lambda j: (0, j)),
        ],
        out_specs=pl.BlockSpec((bsz, tn), lambda j: (0, j)),
        out_shape=jax.ShapeDtypeStruct((bsz, n), F32),
        compiler_params=pltpu.CompilerParams(dimension_semantics=("arbitrary",)),
        name="ada_proj",
    )(c, w, b.reshape(1, n))


def _ffn_kernel(x_ref, ada_ref, gpre_ref, gpost_ref, win_ref, wout_ref, o_ref, g_scr, y_scr,
                *, slot, d_ff, tf):
    tm, d = y_scr.shape
    shift = ada_ref[0, 3 * slot:3 * slot + 1, :]
    scale = ada_ref[0, 3 * slot + 1:3 * slot + 2, :]
    gate = ada_ref[0, 3 * slot + 2:3 * slot + 3, :]
    groups = [pl.ds(i, FFN_ROWS) for i in range(0, tm, FFN_ROWS)]
    hbs = []
    for rows in groups:
        x = x_ref[0, rows, :]
        h = x * _rms_scale(x) * gpre_ref[...]
        hbs.append((h * (1.0 + scale) + shift).astype(BF16))
    for rows, hb in zip(groups, hbs):
        for j in range(d_ff // tf):
            a = jnp.dot(hb, win_ref[:, j * tf:(j + 1) * tf], preferred_element_type=F32)
            b = jnp.dot(hb, win_ref[:, d_ff + j * tf:d_ff + (j + 1) * tf], preferred_element_type=F32)
            g_scr[rows, j * tf:(j + 1) * tf] = (a * _sigmoid(a) * b).astype(BF16)
        ssq = jnp.zeros((FFN_ROWS, 1), F32)
        for j in range(0, d, MXU_TILE):
            y = jnp.dot(g_scr[rows, :], wout_ref[:, j:j + MXU_TILE], preferred_element_type=F32)
            y_scr[rows, j:j + MXU_TILE] = y
            ssq = ssq + jnp.sum(y * y, axis=-1, keepdims=True)
        o_ref[0, rows, :] = x_ref[0, rows, :] + (FFN_RES_W * gate * gpost_ref[...]) * (
            y_scr[rows, :] * lax.rsqrt(ssq * (1.0 / d) + EPS))


def _ffn(x, ada, g_pre, g_post, w_in, w_out, slot):
    bsz, seq, d = x.shape
    d_ff = w_out.shape[0]
    tm = min(FFN_TM, seq)
    tf = FFN_TF
    assert seq % tm == 0 and tm % FFN_ROWS == 0 and d_ff % tf == 0 and d % MXU_TILE == 0
    kern = functools.partial(_ffn_kernel, slot=slot, d_ff=d_ff, tf=tf)
    return pl.pallas_call(
        kern,
        grid=(bsz, seq // tm),
        in_specs=[
            pl.BlockSpec((1, tm, d), lambda b, s: (b, s, 0)),
            pl.BlockSpec((1, ADA_SLOTS, d), lambda b, s: (b, 0, 0)),
            pl.BlockSpec((1, d), lambda b, s: (0, 0)),
            pl.BlockSpec((1, d), lambda b, s: (0, 0)),
            _resident(w_in.shape),
            _resident(w_out.shape),
        ],
        out_specs=pl.BlockSpec((1, tm, d), lambda b, s: (b, s, 0)),
        out_shape=jax.ShapeDtypeStruct(x.shape, F32),
        scratch_shapes=[pltpu.VMEM((tm, d_ff), BF16), pltpu.VMEM((tm, d), F32)],
        compiler_params=pltpu.CompilerParams(
            dimension_semantics=("parallel", "parallel"), vmem_limit_bytes=VMEM_LIMIT),
        name=f"ffn{slot}",
    )(x, ada, g_pre, g_post, w_in, w_out)


_NT = (((1,), (1,)), ((), ()))
_TN = (((0,), (0,)), ((), ()))
PITCH = 2 * SUBLANES


def _block_rows(r0, c):
    return [2 * (r0 + SUBLANES * j) + SUBLANES for j in range(c // SUBLANES)]


def _level_split(nblk, nb):
    odd = [j for j in range(nblk) if (j // nb) % 2 == 1]
    even = [j for j in range(nblk) if (j // nb) % 2 == 0]
    return odd, even


def _hgrn2_scores(a_ref, q_ref, k_ref, v_ref, r0, hd, c):
    nblk = c // SUBLANES
    dk = q_ref.shape[-1] // N_HEADS
    base = _block_rows(r0, c)
    qs = [q_ref[pl.ds(r0 + SUBLANES * j, SUBLANES), pl.ds(hd * dk, dk)] for j in range(nblk)]
    ks = [k_ref[hd, pl.ds(base[j], SUBLANES), :] for j in range(nblk)]
    vs = [v_ref[hd, pl.ds(base[j], SUBLANES), :] for j in range(nblk)]
    acc = [a_ref[hd, pl.ds(base[j], SUBLANES), :] for j in range(nblk)]
    tot = [t[SUBLANES - 1:SUBLANES] for t in acc]
    half = c // 2
    ti = lax.broadcasted_iota(jnp.int32, (half, half), 0)
    si = lax.broadcasted_iota(jnp.int32, (half, half), 1)
    scores = []
    b = SUBLANES
    while b < c:
        nb = b // SUBLANES
        odd, even = _level_split(nblk, nb)
        qc = jnp.concatenate([qs[j] * jnp.exp2(acc[j]) for j in odd], axis=0).astype(BF16)
        kc = jnp.concatenate([ks[j] * jnp.exp2(tot[j // nb] - acc[j]) for j in even], axis=0).astype(BF16)
        sc = lax.dot_general(qc, kc, _NT, preferred_element_type=F32)
        shift = b.bit_length() - 1
        same = lax.shift_right_logical(ti, shift) == lax.shift_right_logical(si, shift)
        scores.append(jnp.where(same, sc, 0.0).astype(BF16))
        acc = [acc[j] + tot[j // nb - 1] if (j // nb) % 2 == 1 else acc[j] for j in range(nblk)]
        tot = [tot[2 * m] + tot[2 * m + 1] for m in range(len(tot) // 2)]
        b *= 2
    a_c = jnp.concatenate(acc, axis=0)
    t_c = tot[0]
    q_state = (jnp.concatenate(qs, axis=0) * jnp.exp2(a_c)).astype(BF16)
    k_state = (jnp.concatenate(ks, axis=0) * jnp.exp2(t_c - a_c)).astype(BF16)
    v_all = jnp.concatenate(vs, axis=0).astype(BF16)
    d_state = lax.dot_general(v_all, k_state, _TN, preferred_element_type=F32)
    return scores, q_state, d_state, jnp.exp2(t_c)


def _hgrn2_leaf(a_ref, q_ref, k_ref, v_ref, r0, hd, c):
    dk = q_ref.shape[-1] // N_HEADS
    o = []
    for j, row in enumerate(_block_rows(r0, c)):
        q = q_ref[pl.ds(r0 + SUBLANES * j, SUBLANES), pl.ds(hd * dk, dk)]
        data = pl.ds(row, SUBLANES)
        acc = a_ref[hd, data, :]
        oj = jnp.sum(q * k_ref[hd, data, :], axis=-1, keepdims=True) * v_ref[hd, data, :]
        for dlt in range(1, SUBLANES):
            sh = pl.ds(row - dlt, SUBLANES)
            w = jnp.exp2(acc - a_ref[hd, sh, :])
            s = jnp.sum(q * w * k_ref[hd, sh, :], axis=-1, keepdims=True)
            oj = oj + s * v_ref[hd, sh, :]
        o.append(oj)
    return jnp.concatenate(o, axis=0)


def _hgrn2_output(scores, q_state, st_bf16, v_ref, r0, hd, c):
    nblk = c // SUBLANES
    base = _block_rows(r0, c)
    vs = [v_ref[hd, pl.ds(base[j], SUBLANES), :] for j in range(nblk)]
    o_inter = lax.dot_general(q_state, st_bf16, _NT, preferred_element_type=F32)
    o = [o_inter[j * SUBLANES:(j + 1) * SUBLANES] for j in range(nblk)]
    b = SUBLANES
    for sc in scores:
        odd, even = _level_split(nblk, b // SUBLANES)
        vc = jnp.concatenate([vs[j] for j in even], axis=0).astype(BF16)
        contrib = jnp.dot(sc, vc, preferred_element_type=F32)
        for i, j in enumerate(odd):
            o[j] = o[j] + contrib[i * SUBLANES:(i + 1) * SUBLANES]
        b *= 2
    return jnp.concatenate(o, axis=0)


def _mixer_kernel(x_ref, ada_ref, gpre_ref, gpost_ref, wmix_ref, convw_ref, convb_ref, wconv_ref,
                  hgn_ref, lbl_ref, whg_ref, wout_ref, o_ref,
                  state_scr, q_scr, k_scr, v_scr, a_scr, p_scr, og_scr, ga_scr, gb_scr, ya_scr, o_scr,
                  y_scr, hb_scr, yain_scr, ob_scr, m_scr, *, layer, chunk):
    tm, d = q_scr.shape
    dk = d // N_HEADS
    nblk = tm // SUBLANES
    n_chunks = tm // chunk
    items = [(ci, hd) for ci in range(n_chunks) for hd in range(N_HEADS)]
    col_tiles = [pl.ds(j, MXU_TILE) for j in range(0, d, MXU_TILE)]
    heads_per_tile = MXU_TILE // dk
    s_idx = pl.program_id(1)

    @pl.when(s_idx == 0)
    def _():
        state_scr[...] = jnp.zeros_like(state_scr)
        zeros = jnp.zeros((N_HEADS, SUBLANES, dk), F32)
        p_scr[:, pl.ds(0, SUBLANES), :] = zeros
        for j in range(nblk):
            pad = pl.ds(PITCH * j, SUBLANES)
            a_scr[:, pad, :] = zeros
            k_scr[:, pad, :] = zeros
            v_scr[:, pad, :] = zeros

    x = x_ref[0]
    shift = ada_ref[0, 3:4, :]
    scale = ada_ref[0, 4:5, :]
    gate = ada_ref[0, 5:6, :]
    h = x * _rms_scale(x) * gpre_ref[...]
    hb_scr[...] = (h * (1.0 + scale) + shift).astype(BF16)

    def proj(g, t):
        cols = pl.ds(g * d + t * MXU_TILE, MXU_TILE)
        return jnp.dot(hb_scr[...], wmix_ref[:, cols], preferred_element_type=F32)

    rowi = lax.broadcasted_iota(jnp.int32, (SUBLANES, 1), 0)

    def store_blocks(ref, t, val, cumulate=False):
        for i in range(heads_per_tile):
            hd = t * heads_per_tile + i
            for j in range(nblk):
                blk = val[j * SUBLANES:(j + 1) * SUBLANES, i * dk:(i + 1) * dk]
                if cumulate:
                    for sh in (1, 2, 4):
                        blk = blk + jnp.where(rowi >= sh, pltpu.roll(blk, sh, 0), 0.0)
                ref[hd, pl.ds(PITCH * j + SUBLANES, SUBLANES), :] = blk

    lbl = lbl_ref[...]
    lbe = jnp.exp(lbl - jnp.max(lbl, axis=0, keepdims=True))
    lbs = lbe / jnp.sum(lbe, axis=0, keepdims=True)
    lb_all = jnp.sum(lbs[:layer + 1], axis=0, keepdims=True)
    for t, cols in enumerate(col_tiles):
        lb = lb_all[:, t * MXU_TILE:(t + 1) * MXU_TILE]
        hq = proj(3, t)
        q_scr[:, cols] = hq * _sigmoid(hq)
        sf = _sigmoid(proj(4, t))
        store_blocks(k_scr, t, (1.0 - lb) * (1.0 - sf))
        store_blocks(a_scr, t, jnp.log2(lb + (1.0 - lb) * sf), cumulate=True)
        store_blocks(v_scr, t, proj(5, t))

    stage1 = {it: _hgrn2_scores(a_scr, q_scr, k_scr, v_scr, it[0] * chunk, it[1], chunk) for it in items}

    for t, cols in enumerate(col_tiles):
        p = proj(1, t) * proj(2, t)
        for i in range(heads_per_tile):
            p_scr[t * heads_per_tile + i, pl.ds(SUBLANES, tm), :] = p[:, i * dk:(i + 1) * dk]
    for t, cols in enumerate(col_tiles):
        conv = []
        for i in range(heads_per_tile):
            hd = t * heads_per_tile + i
            hc = pl.ds(hd * dk, dk)
            conv.append(convb_ref[:, hc]
                        + convw_ref[0:1, hc] * p_scr[hd, pl.ds(SUBLANES - 2, tm), :]
                        + convw_ref[1:2, hc] * p_scr[hd, pl.ds(SUBLANES - 1, tm), :]
                        + convw_ref[2:3, hc] * p_scr[hd, pl.ds(SUBLANES, tm), :])
        yain_scr[:, cols] = (proj(0, t) * jnp.concatenate(conv, axis=1)).astype(BF16)
    p_scr[:, pl.ds(0, SUBLANES), :] = p_scr[:, pl.ds(tm, SUBLANES), :]
    for cols in col_tiles:
        ya_scr[:, cols] = jnp.dot(yain_scr[...], wconv_ref[:, cols], preferred_element_type=F32)

    for ci, hd in items:
        o_scr[pl.ds(ci * chunk, chunk), pl.ds(hd * dk, dk)] = _hgrn2_leaf(
            a_scr, q_scr, k_scr, v_scr, ci * chunk, hd, chunk)
    chunk_state = {}
    for hd in range(N_HEADS):
        st = state_scr[hd]
        for ci in range(n_chunks):
            chunk_state[ci, hd] = st.astype(BF16)
            _, _, d_state, decay = stage1[ci, hd]
            st = st * decay + d_state
        state_scr[hd] = st

    for t, cols in enumerate(col_tiles):
        hg = proj(6, t)
        og_scr[:, cols] = hg * _sigmoid(hg)

    for ci, hd in items:
        scores, q_state, _, _ = stage1[ci, hd]
        rows, cols = pl.ds(ci * chunk, chunk), pl.ds(hd * dk, dk)
        o = o_scr[rows, cols] + _hgrn2_output(
            scores, q_state, chunk_state[ci, hd], v_scr, ci * chunk, hd, chunk)
        ob_scr[rows, cols] = (o * _rms_scale(o) * hgn_ref[:, cols] * og_scr[rows, cols]).astype(BF16)

    for t, cols in enumerate(col_tiles):
        ga_scr[:, cols] = _sigmoid(proj(7, t))
        gb_scr[:, cols] = _sigmoid(proj(8, t))

    for cols in col_tiles:
        yb = jnp.dot(ob_scr[...], whg_ref[:, cols], preferred_element_type=F32)
        m_scr[:, cols] = (ga_scr[:, cols] * ya_scr[:, cols] + gb_scr[:, cols] * yb).astype(BF16)
    ssq = jnp.zeros((tm, 1), F32)
    for cols in col_tiles:
        y = jnp.dot(m_scr[...], wout_ref[:, cols], preferred_element_type=F32)
        y_scr[:, cols] = y
        ssq = ssq + jnp.sum(y * y, axis=-1, keepdims=True)
    o_ref[0] = x + (gate * gpost_ref[...]) * (y_scr[...] * lax.rsqrt(ssq * (1.0 / d) + EPS))


def _mixer(x, ada, g_pre, g_post, w_mix_in, conv_w, conv_b, w_conv_out, hg_norm_g, lb_logits,
           w_hg_out, w_mix_out, layer):
    bsz, seq, d = x.shape
    tm = min(MIX_TM, seq)
    chunk = min(MIX_CHUNK, tm)
    dk = d // N_HEADS
    assert seq % tm == 0 and tm % chunk == 0 and dk == LANES and d % MXU_TILE == 0
    kern = functools.partial(_mixer_kernel, layer=layer, chunk=chunk)
    tile = pltpu.VMEM((tm, d), F32)
    tile_bf16 = pltpu.VMEM((tm, d), BF16)
    padded = pltpu.VMEM((N_HEADS, 2 * tm, dk), F32)
    return pl.pallas_call(
        kern,
        grid=(bsz, seq // tm),
        in_specs=[
            pl.BlockSpec((1, tm, d), lambda b, s: (b, s, 0)),
            pl.BlockSpec((1, ADA_SLOTS, d), lambda b, s: (b, 0, 0)),
            pl.BlockSpec((1, d), lambda b, s: (0, 0)),
            pl.BlockSpec((1, d), lambda b, s: (0, 0)),
            _resident(w_mix_in.shape),
            pl.BlockSpec((CONV_K, d), lambda b, s: (0, 0)),
            pl.BlockSpec((1, d), lambda b, s: (0, 0)),
            _resident(w_conv_out.shape),
            pl.BlockSpec((1, d), lambda b, s: (0, 0)),
            pl.BlockSpec(lb_logits.shape, lambda b, s: (0, 0)),
            _resident(w_hg_out.shape),
            _resident(w_mix_out.shape),
        ],
        out_specs=pl.BlockSpec((1, tm, d), lambda b, s: (b, s, 0)),
        out_shape=jax.ShapeDtypeStruct(x.shape, F32),
        scratch_shapes=[
            pltpu.VMEM((N_HEADS, dk, dk), F32),
            tile, padded, padded, padded,
            pltpu.VMEM((N_HEADS, tm + SUBLANES, dk), F32),
            tile, tile, tile, tile,
            tile, tile,
            tile_bf16, tile_bf16, tile_bf16, tile_bf16,
        ],
        compiler_params=pltpu.CompilerParams(
            dimension_semantics=("arbitrary", "arbitrary"), vmem_limit_bytes=VMEM_LIMIT),
        name="mixer",
    )(x, ada, g_pre, g_post, w_mix_in, conv_w, conv_b, w_conv_out, hg_norm_g, lb_logits,
      w_hg_out, w_mix_out)


def kernel(x, c, w_ada, b_ada, norm_gains, w_ffn1_in, w_ffn1_out, w_mix_in, conv_w, conv_b,
           w_conv_out, hg_norm_g, lb_logits, w_hg_out, w_mix_out, w_ffn2_in, w_ffn2_out):
    depth = w_ada.shape[0]
    bsz, _, d = x.shape
    for l in range(depth):
        ada = _ada(c, w_ada[l], b_ada[l]).reshape(bsz, ADA_SLOTS, d)
        ng = norm_gains[l]
        gain = lambda i: ng[i:i + 1]
        x = _ffn(x, ada, gain(0), gain(1), w_ffn1_in[l].astype(BF16), _weight(w_ffn1_out[l]), 0)
        x = _mixer(x, ada, gain(2), gain(3), _weight(w_mix_in[l]), conv_w[l], conv_b[l:l + 1],
                   _weight(w_conv_out[l]), hg_norm_g[l:l + 1], lb_logits,
                   _weight(w_hg_out[l]), _weight(w_mix_out[l]), l)
        x = _ffn(x, ada, gain(4), gain(5), w_ffn2_in[l].astype(BF16), _weight(w_ffn2_out[l]), 2)
    return x
```

```python
import functools
import math

import jax
import jax.numpy as jnp
from jax import lax
from jax.experimental import pallas as pl
from jax.experimental.pallas import tpu as pltpu

EPS = 1e-6
FFN_RES_W = 0.5
N_HEADS = 8
ADA_SLOTS = 9
CONV_K = 3
NEG_LOG2E = -math.log2(math.e)

F32 = jnp.float32
BF16 = jnp.bfloat16

LANES = 128
SUBLANES = 8
MXU_TILE = 256

FFN_TM = 1024
FFN_ROWS = 512
FFN_TF = 256
MIX_TM = 256
MIX_CHUNK = 128
VMEM_LIMIT = 56 * 1024 * 1024


def _sigmoid(x):
    return 1.0 / (1.0 + jnp.exp2(x * NEG_LOG2E))


def _rms_scale(x):
    return lax.rsqrt(jnp.mean(x * x, axis=-1, keepdims=True) + EPS)


def _resident(shape):
    nd = len(shape)
    return pl.BlockSpec(shape, lambda *_: (0,) * nd, pipeline_mode=pl.Buffered(1))


def _weight(w):
    return jnp.pad(w.astype(BF16), ((0, 0), (0, LANES)))


def _ada_kernel(c_ref, w_ref, b_ref, o_ref):
    c = c_ref[...]
    ca = (c * _sigmoid(c)).astype(BF16)
    o_ref[...] = jnp.dot(ca, w_ref[...].astype(BF16), preferred_element_type=F32) + b_ref[...]


def _ada(c, w, b):
    bsz, d = c.shape
    n = w.shape[1]
    tn = d
    return pl.pallas_call(
        _ada_kernel,
        grid=(n // tn,),
        in_specs=[
            pl.BlockSpec((bsz, d), lambda j: (0, 0)),
            pl.BlockSpec((d, tn), lambda j: (0, j)),
            pl.BlockSpec((1, tn), lambda j: (0, j)),
        ],
        out_specs=pl.BlockSpec((bsz, tn), lambda j: (0, j)),
        out_shape=jax.ShapeDtypeStruct((bsz, n), F32),
        compiler_params=pltpu.CompilerParams(dimension_semantics=("arbitrary",)),
        name="ada_proj",
    )(c, w, b.reshape(1, n))


def _ffn_kernel(x_ref, ada_ref, gpre_ref, gpost_ref, win_ref, wout_ref, o_ref, g_scr, y_scr,
                *, slot, d_ff, tf):
    tm, d = y_scr.shape
    shift = ada_ref[0, 3 * slot:3 * slot + 1, :]
    scale = ada_ref[0, 3 * slot + 1:3 * slot + 2, :]
    gate = ada_ref[0, 3 * slot + 2:3 * slot + 3, :]
    groups = [pl.ds(i, FFN_ROWS) for i in range(0, tm, FFN_ROWS)]
    hbs = []
    for rows in groups:
        x = x_ref[0, rows, :]
        h = x * _rms_scale(x) * gpre_ref[...]
        hbs.append((h * (1.0 + scale) + shift).astype(BF16))
    for rows, hb in zip(groups, hbs):
        for j in range(d_ff // tf):
            a = jnp.dot(hb, win_ref[:, j * tf:(j + 1) * tf], preferred_element_type=F32)
            b = jnp.dot(hb, win_ref[:, d_ff + j * tf:d_ff + (j + 1) * tf], preferred_element_type=F32)
            g_scr[rows, j * tf:(j + 1) * tf] = (a * _sigmoid(a) * b).astype(BF16)
        ssq = jnp.zeros((FFN_ROWS, 1), F32)
        for j in range(0, d, MXU_TILE):
            y = jnp.dot(g_scr[rows, :], wout_ref[:, j:j + MXU_TILE], preferred_element_type=F32)
            y_scr[rows, j:j + MXU_TILE] = y
            ssq = ssq + jnp.sum(y * y, axis=-1, keepdims=True)
        o_ref[0, rows, :] = x_ref[0, rows, :] + (FFN_RES_W * gate * gpost_ref[...]) * (
            y_scr[rows, :] * lax.rsqrt(ssq * (1.0 / d) + EPS))


def _ffn(x, ada, g_pre, g_post, w_in, w_out, slot):
    bsz, seq, d = x.shape
    d_ff = w_out.shape[0]
    tm = min(FFN_TM, seq)
    tf = FFN_TF
    assert seq % tm == 0 and tm % FFN_ROWS == 0 and d_ff % tf == 0 and d % MXU_TILE == 0
    kern = functools.partial(_ffn_kernel, slot=slot, d_ff=d_ff, tf=tf)
    return pl.pallas_call(
        kern,
        grid=(bsz, seq // tm),
        in_specs=[
            pl.BlockSpec((1, tm, d), lambda b, s: (b, s, 0)),
            pl.BlockSpec((1, ADA_SLOTS, d), lambda b, s: (b, 0, 0)),
            pl.BlockSpec((1, d), lambda b, s: (0, 0)),
            pl.BlockSpec((1, d), lambda b, s: (0, 0)),
            _resident(w_in.shape),
            _resident(w_out.shape),
        ],
        out_specs=pl.BlockSpec((1, tm, d), lambda b, s: (b, s, 0)),
        out_shape=jax.ShapeDtypeStruct(x.shape, F32),
        scratch_shapes=[pltpu.VMEM((tm, d_ff), BF16), pltpu.VMEM((tm, d), F32)],
        compiler_params=pltpu.CompilerParams(
            dimension_semantics=("parallel", "parallel"), vmem_limit_bytes=VMEM_LIMIT),
        name=f"ffn{slot}",
    )(x, ada, g_pre, g_post, w_in, w_out)


_NT = (((1,), (1,)), ((), ()))
_TN = (((0,), (0,)), ((), ()))
PITCH = 2 * SUBLANES


def _block_rows(r0, c):
    return [2 * (r0 + SUBLANES * j) + SUBLANES for j in range(c // SUBLANES)]


def _level_split(nblk, nb):
    odd = [j for j in range(nblk) if (j // nb) % 2 == 1]
    even = [j for j in range(nblk) if (j // nb) % 2 == 0]
    return odd, even


def _hgrn2_scores(a_ref, q_ref, k_ref, v_ref, r0, hd, c):
    nblk = c // SUBLANES
    dk = q_ref.shape[-1] // N_HEADS
    base = _block_rows(r0, c)
    qs = [q_ref[pl.ds(r0 + SUBLANES * j, SUBLANES), pl.ds(hd * dk, dk)] for j in range(nblk)]
    ks = [k_ref[hd, pl.ds(base[j], SUBLANES), :] for j in range(nblk)]
    vs = [v_ref[hd, pl.ds(base[j], SUBLANES), :] for j in range(nblk)]
    acc = [a_ref[hd, pl.ds(base[j], SUBLANES), :] for j in range(nblk)]
    tot = [t[SUBLANES - 1:SUBLANES] for t in acc]
    half = c // 2
    ti = lax.broadcasted_iota(jnp.int32, (half, half), 0)
    si = lax.broadcasted_iota(jnp.int32, (half, half), 1)
    scores = []
    b = SUBLANES
    while b < c:
        nb = b // SUBLANES
        odd, even = _level_split(nblk, nb)
        qc = jnp.concatenate([qs[j] * jnp.exp2(acc[j]) for j in odd], axis=0).astype(BF16)
        kc = jnp.concatenate([ks[j] * jnp.exp2(tot[j // nb] - acc[j]) for j in even], axis=0).astype(BF16)
        sc = lax.dot_general(qc, kc, _NT, preferred_element_type=F32)
        shift = b.bit_length() - 1
        same = lax.shift_right_logical(ti, shift) == lax.shift_right_logical(si, shift)
        scores.append(jnp.where(same, sc, 0.0).astype(BF16))
        acc = [acc[j] + tot[j // nb - 1] if (j // nb) % 2 == 1 else acc[j] for j in range(nblk)]
        tot = [tot[2 * m] + tot[2 * m + 1] for m in range(len(tot) // 2)]
        b *= 2
    a_c = jnp.concatenate(acc, axis=0)
    t_c = tot[0]
    q_state = (jnp.concatenate(qs, axis=0) * jnp.exp2(a_c)).astype(BF16)
    k_state = (jnp.concatenate(ks, axis=0) * jnp.exp2(t_c - a_c)).astype(BF16)
    v_all = jnp.concatenate(vs, axis=0).astype(BF16)
    d_state = lax.dot_general(v_all, k_state, _TN, preferred_element_type=F32)
    return scores, q_state, d_state, jnp.exp2(t_c)


def _hgrn2_leaf(a_ref, q_ref, k_ref, v_ref, r0, hd, c):
    dk = q_ref.shape[-1] // N_HEADS
    o = []
    for j, row in enumerate(_block_rows(r0, c)):
        q = q_ref[pl.ds(r0 + SUBLANES * j, SUBLANES), pl.ds(hd * dk, dk)]
        data = pl.ds(row, SUBLANES)
        acc = a_ref[hd, data, :]
        oj = jnp.sum(q * k_ref[hd, data, :], axis=-1, keepdims=True) * v_ref[hd, data, :]
        for dlt in range(1, SUBLANES):
            sh = pl.ds(row - dlt, SUBLANES)
            w = jnp.exp2(acc - a_ref[hd, sh, :])
            s = jnp.sum(q * w * k_ref[hd, sh, :], axis=-1, keepdims=True)
            oj = oj + s * v_ref[hd, sh, :]
        o.append(oj)
    return jnp.concatenate(o, axis=0)


def _hgrn2_output(scores, q_state, st_bf16, v_ref, r0, hd, c):
    nblk = c // SUBLANES
    base = _block_rows(r0, c)
    vs = [v_ref[hd, pl.ds(base[j], SUBLANES), :] for j in range(nblk)]
    o_inter = lax.dot_general(q_state, st_bf16, _NT, preferred_element_type=F32)
    o = [o_inter[j * SUBLANES:(j + 1) * SUBLANES] for j in range(nblk)]
    b = SUBLANES
    for sc in scores:
        odd, even = _level_split(nblk, b // SUBLANES)
        vc = jnp.concatenate([vs[j] for j in even], axis=0).astype(BF16)
        contrib = jnp.dot(sc, vc, preferred_element_type=F32)
        for i, j in enumerate(odd):
            o[j] = o[j] + contrib[i * SUBLANES:(i + 1) * SUBLANES]
        b *= 2
    return jnp.concatenate(o, axis=0)


def _mixer_kernel(x_ref, ada_ref, gpre_ref, gpost_ref, wmix_ref, convw_ref, convb_ref, wconv_ref,
                  hgn_ref, lbl_ref, whg_ref, wout_ref, o_ref,
                  state_scr, q_scr, k_scr, v_scr, a_scr, p_scr, og_scr, ga_scr, gb_scr, ya_scr, o_scr,
                  y_scr, hb_scr, yain_scr, ob_scr, m_scr, *, layer, chunk):
    tm, d = q_scr.shape
    dk = d // N_HEADS
    nblk = tm // SUBLANES
    n_chunks = tm // chunk
    items = [(ci, hd) for ci in range(n_chunks) for hd in range(N_HEADS)]
    col_tiles = [pl.ds(j, MXU_TILE) for j in range(0, d, MXU_TILE)]
    heads_per_tile = MXU_TILE // dk
    s_idx = pl.program_id(1)

    @pl.when(s_idx == 0)
    def _():
        state_scr[...] = jnp.zeros_like(state_scr)
        zeros = jnp.zeros((N_HEADS, SUBLANES, dk), F32)
        p_scr[:, pl.ds(0, SUBLANES), :] = zeros
        for j in range(nblk):
            pad = pl.ds(PITCH * j, SUBLANES)
            a_scr[:, pad, :] = zeros
            k_scr[:, pad, :] = zeros
            v_scr[:, pad, :] = zeros

    x = x_ref[0]
    shift = ada_ref[0, 3:4, :]
    scale = ada_ref[0, 4:5, :]
    gate = ada_ref[0, 5:6, :]
    h = x * _rms_scale(x) * gpre_ref[...]
    hb_scr[...] = (h * (1.0 + scale) + shift).astype(BF16)

    def proj(g, t):
        cols = pl.ds(g * d + t * MXU_TILE, MXU_TILE)
        return jnp.dot(hb_scr[...], wmix_ref[:, cols], preferred_element_type=F32)

    rowi = lax.broadcasted_iota(jnp.int32, (SUBLANES, 1), 0)

    def store_blocks(ref, t, val, cumulate=False):
        for i in range(heads_per_tile):
            hd = t * heads_per_tile + i
            for j in range(nblk):
                blk = val[j * SUBLANES:(j + 1) * SUBLANES, i * dk:(i + 1) * dk]
                if cumulate:
                    for sh in (1, 2, 4):
                        blk = blk + jnp.where(rowi >= sh, pltpu.roll(blk, sh, 0), 0.0)
                ref[hd, pl.ds(PITCH * j + SUBLANES, SUBLANES), :] = blk

    lbl = lbl_ref[...]
    lbe = jnp.exp(lbl - jnp.max(lbl, axis=0, keepdims=True))
    lbs = lbe / jnp.sum(lbe, axis=0, keepdims=True)
    lb_all = jnp.sum(lbs[:layer + 1], axis=0, keepdims=True)
    for t, cols in enumerate(col_tiles):
        lb = lb_all[:, t * MXU_TILE:(t + 1) * MXU_TILE]
        hq = proj(3, t)
        q_scr[:, cols] = hq * _sigmoid(hq)
        sf = _sigmoid(proj(4, t))
        store_blocks(k_scr, t, (1.0 - lb) * (1.0 - sf))
        store_blocks(a_scr, t, jnp.log2(lb + (1.0 - lb) * sf), cumulate=True)
        store_blocks(v_scr, t, proj(5, t))

    stage1 = {it: _hgrn2_scores(a_scr, q_scr, k_scr, v_scr, it[0] * chunk, it[1], chunk) for it in items}

    for t, cols in enumerate(col_tiles):
        p = proj(1, t) * proj(2, t)
        for i in range(heads_per_tile):
            p_scr[t * heads_per_tile + i, pl.ds(SUBLANES, tm), :] = p[:, i * dk:(i + 1) * dk]
    for t, cols in enumerate(col_tiles):
        conv = []
        for i in range(heads_per_tile):
            hd = t * heads_per_tile + i
            hc = pl.ds(hd * dk, dk)
            conv.append(convb_ref[:, hc]
                        + convw_ref[0:1, hc] * p_scr[hd, pl.ds(SUBLANES - 2, tm), :]
                        + convw_ref[1:2, hc] * p_scr[hd, pl.ds(SUBLANES - 1, tm), :]
                        + convw_ref[2:3, hc] * p_scr[hd, pl.ds(SUBLANES, tm), :])
        yain_scr[:, cols] = (proj(0, t) * jnp.concatenate(conv, axis=1)).astype(BF16)
    p_scr[:, pl.ds(0, SUBLANES), :] = p_scr[:, pl.ds(tm, SUBLANES), :]
    for cols in col_tiles:
        ya_scr[:, cols] = jnp.dot(yain_scr[...], wconv_ref[:, cols], preferred_element_type=F32).astype(BF16)

    for ci, hd in items:
        o_scr[pl.ds(ci * chunk, chunk), pl.ds(hd * dk, dk)] = _hgrn2_leaf(
            a_scr, q_scr, k_scr, v_scr, ci * chunk, hd, chunk)
    chunk_state = {}
    for hd in range(N_HEADS):
        st = state_scr[hd]
        for ci in range(n_chunks):
            chunk_state[ci, hd] = st.astype(BF16)
            _, _, d_state, decay = stage1[ci, hd]
            st = st * decay + d_state
        state_scr[hd] = st

    for t, cols in enumerate(col_tiles):
        hg = proj(6, t)
        og_scr[:, cols] = (hg * _sigmoid(hg)).astype(BF16)

    for ci, hd in items:
        scores, q_state, _, _ = stage1[ci, hd]
        rows, cols = pl.ds(ci * chunk, chunk), pl.ds(hd * dk, dk)
        o = o_scr[rows, cols] + _hgrn2_output(
            scores, q_state, chunk_state[ci, hd], v_scr, ci * chunk, hd, chunk)
        ob_scr[rows, cols] = (o * _rms_scale(o) * hgn_ref[:, cols]
                              * og_scr[rows, cols].astype(F32)).astype(BF16)

    for t, cols in enumerate(col_tiles):
        ga_scr[:, cols] = _sigmoid(proj(7, t)).astype(BF16)
        gb_scr[:, cols] = _sigmoid(proj(8, t)).astype(BF16)

    for cols in col_tiles:
        yb = jnp.dot(ob_scr[...], whg_ref[:, cols], preferred_element_type=F32)
        m_scr[:, cols] = (ga_scr[:, cols].astype(F32) * ya_scr[:, cols].astype(F32)
                          + gb_scr[:, cols].astype(F32) * yb).astype(BF16)
    ssq = jnp.zeros((tm, 1), F32)
    for cols in col_tiles:
        y = jnp.dot(m_scr[...], wout_ref[:, cols], preferred_element_type=F32)
        y_scr[:, cols] = y
        ssq = ssq + jnp.sum(y * y, axis=-1, keepdims=True)
    o_ref[0] = x + (gate * gpost_ref[...]) * (y_scr[...] * lax.rsqrt(ssq * (1.0 / d) + EPS))


def _mixer(x, ada, g_pre, g_post, w_mix_in, conv_w, conv_b, w_conv_out, hg_norm_g, lb_logits,
           w_hg_out, w_mix_out, layer):
    bsz, seq, d = x.shape
    tm = min(MIX_TM, seq)
    chunk = min(MIX_CHUNK, tm)
    dk = d // N_HEADS
    assert seq % tm == 0 and tm % chunk == 0 and dk == LANES and d % MXU_TILE == 0
    kern = functools.partial(_mixer_kernel, layer=layer, chunk=chunk)
    tile = pltpu.VMEM((tm, d), F32)
    tile_bf16 = pltpu.VMEM((tm, d), BF16)
    padded = pltpu.VMEM((N_HEADS, 2 * tm, dk), F32)
    return pl.pallas_call(
        kern,
        grid=(bsz, seq // tm),
        in_specs=[
            pl.BlockSpec((1, tm, d), lambda b, s: (b, s, 0)),
            pl.BlockSpec((1, ADA_SLOTS, d), lambda b, s: (b, 0, 0)),
            pl.BlockSpec((1, d), lambda b, s: (0, 0)),
            pl.BlockSpec((1, d), lambda b, s: (0, 0)),
            _resident(w_mix_in.shape),
            pl.BlockSpec((CONV_K, d), lambda b, s: (0, 0)),
            pl.BlockSpec((1, d), lambda b, s: (0, 0)),
            _resident(w_conv_out.shape),
            pl.BlockSpec((1, d), lambda b, s: (0, 0)),
            pl.BlockSpec(lb_logits.shape, lambda b, s: (0, 0)),
            _resident(w_hg_out.shape),
            _resident(w_mix_out.shape),
        ],
        out_specs=pl.BlockSpec((1, tm, d), lambda b, s: (b, s, 0)),
        out_shape=jax.ShapeDtypeStruct(x.shape, F32),
        scratch_shapes=[
            pltpu.VMEM((N_HEADS, dk, dk), F32),
            tile, padded, padded, padded,
            pltpu.VMEM((N_HEADS, tm + SUBLANES, dk), F32),
            tile_bf16, tile_bf16, tile_bf16, tile_bf16,
            tile, tile,
            tile_bf16, tile_bf16, tile_bf16, tile_bf16,
        ],
        compiler_params=pltpu.CompilerParams(
            dimension_semantics=("arbitrary", "arbitrary"), vmem_limit_bytes=VMEM_LIMIT),
        name="mixer",
    )(x, ada, g_pre, g_post, w_mix_in, conv_w, conv_b, w_conv_out, hg_norm_g, lb_logits,
      w_hg_out, w_mix_out)


def kernel(x, c, w_ada, b_ada, norm_gains, w_ffn1_in, w_ffn1_out, w_mix_in, conv_w, conv_b,
           w_conv_out, hg_norm_g, lb_logits, w_hg_out, w_mix_out, w_ffn2_in, w_ffn2_out):
    depth = w_ada.shape[0]
    bsz, _, d = x.shape
    for l in range(depth):
        ada = _ada(c, w_ada[l], b_ada[l]).reshape(bsz, ADA_SLOTS, d)
        ng = norm_gains[l]
        gain = lambda i: ng[i:i + 1]
        x = _ffn(x, ada, gain(0), gain(1), w_ffn1_in[l].astype(BF16), _weight(w_ffn1_out[l]), 0)
        x = _mixer(x, ada, gain(2), gain(3), _weight(w_mix_in[l]), conv_w[l], conv_b[l:l + 1],
                   _weight(w_conv_out[l]), hg_norm_g[l:l + 1], lb_logits,
                   _weight(w_hg_out[l]), _weight(w_mix_out[l]), l)
        x = _ffn(x, ada, gain(4), gain(5), w_ffn2_in[l].astype(BF16), _weight(w_ffn2_out[l]), 2)
    return x
```
